```python
import math
import jax, jax.numpy as jnp
from jax import lax
import numpy as np

D_MODEL = 2048
BATCH = 1
SEQ = 8192
DEPTH = 4

N_MIXERS = 3
EPS = 1e-6
PLE_DIM = 256
D_FF = -(-8 * D_MODEL // (3 * 256)) * 256

SB_HEAD_DIM = 128
SB_HEADS = D_MODEL // SB_HEAD_DIM
Q_BLOCK = 128

S5_GROUP = 16
S5_GROUPS = D_MODEL // S5_GROUP
S5_STATE = 64
S5_CHUNK = 128

ML_HEADS = 8
ML_QK_DIM = D_MODEL // ML_HEADS // 2
ML_V_DIM = D_MODEL // ML_HEADS
ML_NQK = ML_HEADS * ML_QK_DIM
ML_IN = 2 * ML_NQK + 2 * D_MODEL + 2 * ML_HEADS
ML_CHUNK = 128

N_SB = (DEPTH + 2) // N_MIXERS
N_S5 = (DEPTH + 1) // N_MIXERS
N_ML = DEPTH // N_MIXERS

kernel_name = "hybrid_sb_s5_mlstm_trunk"


def rms_norm(x, g):
    xf = x.astype(jnp.float32)
    y = xf * lax.rsqrt(jnp.mean(xf * xf, axis=-1, keepdims=True) + EPS)
    return (y * g.astype(jnp.float32)).astype(x.dtype)


def split_heads(t, n, d):
    b, l, _ = t.shape
    return t.reshape(b, l, n, d).transpose(0, 2, 1, 3)


def stick_breaking_attention(h, w_in, q_gain, k_gain, w_out):
    bsz, seq, _ = h.shape
    q, k, v = jnp.split(h @ w_in, 3, axis=-1)
    q = rms_norm(split_heads(q, SB_HEADS, SB_HEAD_DIM), q_gain).astype(jnp.float32)
    k = rms_norm(split_heads(k, SB_HEADS, SB_HEAD_DIM), k_gain).astype(jnp.float32)
    v = split_heads(v, SB_HEADS, SB_HEAD_DIM).astype(jnp.float32)
    q = q * (SB_HEAD_DIM ** -0.5)
    nb = seq // Q_BLOCK
    q_blocks = q.reshape(bsz, SB_HEADS, nb, Q_BLOCK, SB_HEAD_DIM).transpose(2, 0, 1, 3, 4)
    key_pos = jnp.arange(seq)

    def block(args):
        qb, b = args
        z = jnp.einsum('bhqd,bhkd->bhqk', qb, k)
        q_pos = b * Q_BLOCK + jnp.arange(Q_BLOCK)
        strict = key_pos[None, :] < q_pos[:, None]
        log_1m = jnp.where(strict, jax.nn.log_sigmoid(-z), 0.0)
        rest = lax.cumsum(log_1m, axis=3, reverse=True) - log_1m
        a = jnp.where(strict, jnp.exp(jax.nn.log_sigmoid(z) + rest), 0.0)
        return jnp.einsum('bhqk,bhkd->bhqd', a, v)

    o = lax.map(block, (q_blocks, jnp.arange(nb)))
    o = o.transpose(1, 0, 3, 2, 4).reshape(bsz, seq, D_MODEL).astype(h.dtype)
    return o @ w_out


def s5_layer(h, w_in, lam_re, lam_im, log_dt, b_re, b_im, c_re, c_im, d_skip, w_glu, w_out):
    bsz, seq, _ = h.shape
    f32 = jnp.float32
    u = (h @ w_in).astype(f32).reshape(bsz, seq, S5_GROUPS, S5_GROUP)
    dt = jnp.exp(log_dt.astype(f32))[:, None]
    lr, li = lam_re.astype(f32), lam_im.astype(f32)
    mag = jnp.exp(lr * dt)
    ar, ai = mag * jnp.cos(li * dt), mag * jnp.sin(li * dt)
    den = lr * lr + li * li
    cr = ((ar - 1.0) * lr + ai * li) / den
    ci = (ai * lr - (ar - 1.0) * li) / den
    br, bi = b_re.astype(f32), b_im.astype(f32)
    bbr = cr[..., None] * br - ci[..., None] * bi
    bbi = cr[..., None] * bi + ci[..., None] * br
    cre, cim, dsk = c_re.astype(f32), c_im.astype(f32), d_skip.astype(f32)

    nc = seq // S5_CHUNK
    u_c = u.reshape(bsz, nc, S5_CHUNK, S5_GROUPS, S5_GROUP).transpose(1, 2, 0, 3, 4)
    a_r = jnp.broadcast_to(ar, (S5_CHUNK, bsz, S5_GROUPS, S5_STATE))
    a_i = jnp.broadcast_to(ai, (S5_CHUNK, bsz, S5_GROUPS, S5_STATE))

    def combine(e1, e2):
        a1r, a1i, x1r, x1i = e1
        a2r, a2i, x2r, x2i = e2
        return (a2r * a1r - a2i * a1i, a2r * a1i + a2i * a1r,
                a2r * x1r - a2i * x1i + x2r, a2r * x1i + a2i * x1r + x2i)

    def chunk_step(carry, uc):
        sr, si = carry
        bur = jnp.einsum('gpc,tbgc->tbgp', bbr, uc)
        bui = jnp.einsum('gpc,tbgc->tbgp', bbi, uc)
        pr, pi, xr, xi = lax.associative_scan(combine, (a_r, a_i, bur, bui), axis=0)
        xr = xr + pr * sr - pi * si
        xi = xi + pr * si + pi * sr
        y = (jnp.einsum('gcp,tbgp->tbgc', cre, xr) - jnp.einsum('gcp,tbgp->tbgc', cim, xi)
             + dsk * uc)
        return (xr[-1], xi[-1]), y

    init = (jnp.zeros((bsz, S5_GROUPS, S5_STATE), f32), jnp.zeros((bsz, S5_GROUPS, S5_STATE), f32))
    _, y = lax.scan(chunk_step, init, u_c)
    y = y.transpose(2, 0, 1, 3, 4).reshape(bsz, seq, D_MODEL).astype(h.dtype)
    g = jax.nn.gelu(y)
    g = g * jax.nn.sigmoid(g @ w_glu)
    return g @ w_out


def mlstm_layer(h, w_in, gate_bias, head_gain, w_out):
    bsz, seq, _ = h.shape
    f32 = jnp.float32
    proj = h @ w_in
    q, k, v, o, gi = jnp.split(proj, [ML_NQK, 2 * ML_NQK, 2 * ML_NQK + D_MODEL, 2 * ML_NQK + 2 * D_MODEL], axis=-1)
    q = split_heads(q, ML_HEADS, ML_QK_DIM).astype(f32)
    k = split_heads(k, ML_HEADS, ML_QK_DIM).astype(f32) * (ML_QK_DIM ** -0.5)
    v = split_heads(v, ML_HEADS, ML_V_DIM).astype(f32)
    gates = gi.astype(f32) + gate_bias.astype(f32)
    i_pre = gates[..., :ML_HEADS].transpose(0, 2, 1)
    log_f = jax.nn.log_sigmoid(gates[..., ML_HEADS:]).transpose(0, 2, 1)
    nc = seq // ML_CHUNK

    def to_chunks(t):
        t = t.reshape(bsz, ML_HEADS, nc, ML_CHUNK, *t.shape[3:])
        return jnp.moveaxis(t, 2, 0)

    causal = jnp.tril(jnp.ones((ML_CHUNK, ML_CHUNK), dtype=bool))

    def step(carry, xs):
        c_st, n_st, m_st = carry
        qc, kc, vc, ic, fc = xs
        bcum = jnp.cumsum(fc, axis=-1)
        dmat = jnp.where(causal, bcum[..., :, None] - bcum[..., None, :] + ic[..., None, :], -jnp.inf)
        inter = bcum + m_st[..., None]
        m_row = jnp.maximum(jnp.max(dmat, axis=-1), inter)
        w = jnp.exp(dmat - m_row[..., None])
        w_inter = jnp.exp(inter - m_row)
        s = jnp.einsum('bhtd,bhsd->bhts', qc, kc) * w
        num = (jnp.einsum('bhts,bhsv->bhtv', s, vc)
               + w_inter[..., None] * jnp.einsum('bhtd,bhdv->bhtv', qc, c_st))
        den = jnp.sum(s, axis=-1) + w_inter * jnp.einsum('bhtd,bhd->bht', qc, n_st)
        h_t = num / jnp.maximum(jnp.abs(den), jnp.exp(-m_row))[..., None]
        b_last = bcum[..., -1]
        dec = b_last[..., None] - bcum + ic
        m_new = jnp.maximum(b_last + m_st, jnp.max(dec, axis=-1))
        wk = jnp.exp(dec - m_new[..., None])
        carry_scale = jnp.exp(b_last + m_st - m_new)
        c_new = carry_scale[..., None, None] * c_st + jnp.einsum('bhsd,bhsv->bhdv', kc * wk[..., None], vc)
        n_new = carry_scale[..., None] * n_st + jnp.einsum('bhs,bhsd->bhd', wk, kc)
        return (c_new, n_new, m_new), h_t

    init = (jnp.zeros((bsz, ML_HEADS, ML_QK_DIM, ML_V_DIM), f32),
            jnp.zeros((bsz, ML_HEADS, ML_QK_DIM), f32),
            jnp.zeros((bsz, ML_HEADS), f32))
    _, hs = lax.scan(step, init, (to_chunks(q), to_chunks(k), to_chunks(v), to_chunks(i_pre), to_chunks(log_f)))
    hs = jnp.moveaxis(hs, 0, 2).reshape(bsz, ML_HEADS, seq, ML_V_DIM).transpose(0, 2, 1, 3)
    hs = rms_norm(hs, head_gain).reshape(bsz, seq, D_MODEL).astype(h.dtype)
    hs = jax.nn.sigmoid(o) * hs
    return hs @ w_out


def swiglu(h, w_gate, w_up, w_down):
    return (jax.nn.silu(h @ w_gate) * (h @ w_up)) @ w_down


def setup_inputs(seed: int = 0) -> dict:
    key = jax.random.key(seed)
    ks = iter(jax.random.split(key, 40))

    def nrm(shape, scale):
        return jax.random.normal(next(ks), shape, jnp.float32) * scale

    D = D_MODEL
    G, P, GS = S5_GROUPS, S5_STATE, S5_GROUP
    n_idx = jnp.arange(P, dtype=jnp.float32)
    f_bias = jnp.linspace(3.0, 6.0, ML_HEADS, dtype=jnp.float32)
    return {
        "x": nrm((BATCH, SEQ, D), 1.0),
        "p": nrm((DEPTH, BATCH, SEQ, PLE_DIM), 1.0),
        "norm_mix": 1.0 + nrm((DEPTH, D), 0.02),
        "norm_ffn": 1.0 + nrm((DEPTH, D), 0.02),
        "sb_w_in": nrm((N_SB, D, 3 * D), D ** -0.5),
        "sb_q_gain": 1.0 + nrm((N_SB, SB_HEAD_DIM), 0.02),
        "sb_k_gain": 1.0 + nrm((N_SB, SB_HEAD_DIM), 0.02),
        "sb_w_out": nrm((N_SB, D, D), D ** -0.5),
        "s5_w_in": nrm((N_S5, D, D), D ** -0.5),
        "s5_lam_re": -0.5 * jnp.exp(nrm((N_S5, G, P), 0.05)),
        "s5_lam_im": math.pi * n_idx + nrm((N_S5, G, P), 0.05),
        "s5_log_dt": jax.random.uniform(next(ks), (N_S5, G), jnp.float32, math.log(1e-3), math.log(1e-1)),
        "s5_b_re": nrm((N_S5, G, P, GS), (2 * GS) ** -0.5),
        "s5_b_im": nrm((N_S5, G, P, GS), (2 * GS) ** -0.5),
        "s5_c_re": nrm((N_S5, G, GS, P), (2 * P) ** -0.5),
        "s5_c_im": nrm((N_S5, G, GS, P), (2 * P) ** -0.5),
        "s5_d": nrm((N_S5, G, GS), 1.0),
        "s5_w_glu": nrm((N_S5, D, D), D ** -0.5),
        "s5_w_out": nrm((N_S5, D, D), D ** -0.5),
        "ml_w_in": nrm((N_ML, D, ML_IN), D ** -0.5),
        "ml_gate_bias": jnp.concatenate([jnp.zeros((ML_HEADS,), jnp.float32), f_bias])[None, :] + nrm((N_ML, 2 * ML_HEADS), 0.1),
        "ml_head_gain": 1.0 + nrm((N_ML, ML_HEADS, ML_V_DIM), 0.02),
        "ml_w_out": nrm((N_ML, D, D), D ** -0.5),
        "ffn_w_gate": nrm((DEPTH, D, D_FF), D ** -0.5),
        "ffn_w_up": nrm((DEPTH, D, D_FF), D ** -0.5),
        "ffn_w_down": nrm((DEPTH, D_FF, D), D_FF ** -0.5),
        "ple_norm": 1.0 + nrm((DEPTH, D), 0.02),
        "ple_w_gate": nrm((DEPTH, D, D), D ** -0.5),
        "ple_w_proj": nrm((DEPTH, PLE_DIM, D), PLE_DIM ** -0.5),
    }


def reference(x, p, norm_mix, norm_ffn, sb_w_in, sb_q_gain, sb_k_gain, sb_w_out,
              s5_w_in, s5_lam_re, s5_lam_im, s5_log_dt, s5_b_re, s5_b_im, s5_c_re, s5_c_im,
              s5_d, s5_w_glu, s5_w_out, ml_w_in, ml_gate_bias, ml_head_gain, ml_w_out,
              ffn_w_gate, ffn_w_up, ffn_w_down, ple_norm, ple_w_gate, ple_w_proj):
    for i in range(DEPTH):
        kind, j = i % N_MIXERS, i // N_MIXERS
        hn = rms_norm(x, norm_mix[i])
        if kind == 0:
            y = stick_breaking_attention(hn, sb_w_in[j], sb_q_gain[j], sb_k_gain[j], sb_w_out[j])
        elif kind == 1:
            y = s5_layer(hn, s5_w_in[j], s5_lam_re[j], s5_lam_im[j], s5_log_dt[j], s5_b_re[j], s5_b_im[j],
                         s5_c_re[j], s5_c_im[j], s5_d[j], s5_w_glu[j], s5_w_out[j])
        else:
            y = mlstm_layer(hn, ml_w_in[j], ml_gate_bias[j], ml_head_gain[j], ml_w_out[j])
        x = x + y
        x = x + swiglu(rms_norm(x, norm_ffn[i]), ffn_w_gate[i], ffn_w_up[i], ffn_w_down[i])
        gate = jax.nn.sigmoid(rms_norm(x, ple_norm[i]) @ ple_w_gate[i])
        x = x + gate * (p[i] @ ple_w_proj[i])
    return x
```

```python
import functools
import math

import jax
import jax.numpy as jnp
from jax import lax
from jax.experimental import pallas as pl
from jax.experimental.pallas import tpu as pltpu

F32 = jnp.float32
BF16 = jnp.bfloat16

EPS = 1e-6
LANES = 128
SUBLANES = 8
VMEM_LIMIT = 56 * 1024 * 1024

SB_HEAD_DIM = 128
S5_GROUP = 16
S5_STATE = 64
ML_HEADS = 8
EXP_ZERO_BELOW = -104.0


def _cparams(*sem):
    return pltpu.CompilerParams(dimension_semantics=sem, vmem_limit_bytes=VMEM_LIMIT)


def _const_spec(shape):
    nd = len(shape)
    return pl.BlockSpec(shape, lambda *_: (0,) * nd)


def _rms_to(dst_ref, x_ref, g_ref, chunk=256):
    rows = x_ref.shape[0]
    chunk = min(chunk, rows)

    def body(c, _):
        r = pl.multiple_of(c * chunk, chunk)
        x = x_ref[pl.ds(r, chunk), :]
        ms = jnp.mean(x * x, axis=-1, keepdims=True)
        dst_ref[pl.ds(r, chunk), :] = (x * lax.rsqrt(ms + EPS) * g_ref[...]).astype(dst_ref.dtype)
        return 0

    lax.fori_loop(0, rows // chunk, body, 0)


def _log_sigmoid(z):
    return jnp.minimum(z, 0.0) - jnp.log1p(jnp.exp(-jnp.abs(z)))


def _sigmoid(z):
    return 1.0 / (1.0 + jnp.exp(-z))


def _norm_matmul_kernel(x_ref, g_ref, w_ref, o_ref, xn_ref):
    @pl.when(pl.program_id(1) == 0)
    def _():
        _rms_to(xn_ref, x_ref, g_ref)

    o_ref[...] = jnp.dot(xn_ref[...], w_ref[...], preferred_element_type=F32).astype(o_ref.dtype)


def norm_matmul(x, g, w, out_dtype, *, tm=512, tn=512, name="norm_matmul"):
    m, k = x.shape
    n = w.shape[1]
    tn = min(tn, n)
    assert m % tm == 0 and n % tn == 0
    return pl.pallas_call(
        _norm_matmul_kernel,
        out_shape=jax.ShapeDtypeStruct((m, n), out_dtype),
        grid=(m // tm, n // tn),
        in_specs=[
            pl.BlockSpec((tm, k), lambda i, j: (i, 0)),
            pl.BlockSpec((1, k), lambda i, j: (0, 0)),
            pl.BlockSpec((k, tn), lambda i, j: (0, j)),
        ],
        out_specs=pl.BlockSpec((tm, tn), lambda i, j: (i, j)),
        scratch_shapes=[pltpu.VMEM((tm, k), BF16)],
        compiler_params=_cparams("parallel", "arbitrary"),
        name=name,
    )(x, g.reshape(1, k), w)


def _sb_inproj_kernel(x_ref, g_ref, w_ref, hg_ref, o_ref, xn_ref, *, n_qk_tiles, n_q_tiles):
    j = pl.program_id(1)

    @pl.when(j == 0)
    def _():
        _rms_to(xn_ref, x_ref, g_ref)

    y = jnp.dot(xn_ref[...], w_ref[...], preferred_element_type=F32)
    tn = y.shape[1]

    @pl.when(j < n_qk_tiles)
    def _():
        scale = jnp.where(j < n_q_tiles, SB_HEAD_DIM ** -0.5, 1.0).astype(F32)
        for h in range(tn // SB_HEAD_DIM):
            blk = y[:, h * SB_HEAD_DIM:(h + 1) * SB_HEAD_DIM]
            ms = jnp.mean(blk * blk, axis=-1, keepdims=True)
            nb = blk * lax.rsqrt(ms + EPS) * hg_ref[...]
            o_ref[:, h * SB_HEAD_DIM:(h + 1) * SB_HEAD_DIM] = (nb * scale).astype(o_ref.dtype)

    @pl.when(j >= n_qk_tiles)
    def _():
        o_ref[...] = y.astype(o_ref.dtype)


def sb_inproj(x, g, w, q_gain, k_gain, *, tm=512, tn=512):
    m, k = x.shape
    n = w.shape[1]
    d = n // 3
    n_q_tiles = d // tn
    hg = jnp.stack([q_gain, k_gain]).reshape(2, 1, SB_HEAD_DIM)
    kern = functools.partial(_sb_inproj_kernel, n_qk_tiles=2 * n_q_tiles, n_q_tiles=n_q_tiles)
    return pl.pallas_call(
        kern,
        out_shape=jax.ShapeDtypeStruct((m, n), BF16),
        grid=(m // tm, n // tn),
        in_specs=[
            pl.BlockSpec((tm, k), lambda i, j: (i, 0)),
            pl.BlockSpec((1, k), lambda i, j: (0, 0)),
            pl.BlockSpec((k, tn), lambda i, j: (0, j)),
            pl.BlockSpec((None, 1, SB_HEAD_DIM), lambda i, j: (jnp.minimum(j // n_q_tiles, 1), 0, 0)),
        ],
        out_specs=pl.BlockSpec((tm, tn), lambda i, j: (i, j)),
        scratch_shapes=[pltpu.VMEM((tm, k), BF16)],
        compiler_params=_cparams("parallel", "arbitrary"),
        name="sb_inproj",
    )(x, g.reshape(1, k), w, hg)


def _sb_attn_kernel(q_ref, k_ref, v_ref, o_ref, *, blk):
    qi = pl.program_id(1)
    q = q_ref[...]
    row = lax.broadcasted_iota(jnp.int32, (blk, blk), 0)
    col = lax.broadcasted_iota(jnp.int32, (blk, blk), 1)
    strict = col < row
    suffix = (row >= col).astype(BF16)

    def block(kb, carry, acc, masked):
        r = pl.multiple_of(kb * blk, blk)
        k = k_ref[pl.ds(r, blk), :]
        v = v_ref[pl.ds(r, blk), :]
        z = lax.dot_general(q, k, (((1,), (1,)), ((), ())), preferred_element_type=F32)
        lg = _log_sigmoid(-z)
        if masked:
            lg = jnp.where(strict, lg, 0.0)
        hi = lg.astype(BF16)
        lo = (lg - hi.astype(F32)).astype(BF16)
        csum = (jnp.dot(hi, suffix, preferred_element_type=F32)
                + jnp.dot(lo, suffix, preferred_element_type=F32))
        a = jnp.exp(z + csum + carry)
        if masked:
            a = jnp.where(strict, a, 0.0)
        acc = acc + jnp.dot(a.astype(BF16), v, preferred_element_type=F32)
        carry = carry + csum[:, 0:1]
        return carry, acc

    carry0 = jnp.zeros((blk, 1), F32)
    acc0 = jnp.zeros((blk, q.shape[1]), F32)
    carry, acc = block(qi, carry0, acc0, True)

    def cond(state):
        i, carry, _ = state
        return jnp.logical_and(i < qi, jnp.max(carry) > EXP_ZERO_BELOW)

    def body(state):
        i, carry, acc = state
        carry, acc = block(qi - 1 - i, carry, acc, False)
        return i + 1, carry, acc

    _, _, acc = lax.while_loop(cond, body, (jnp.int32(0), carry, acc))
    o_ref[...] = acc.astype(o_ref.dtype)


def sb_attention(qkv, *, blk=256):
    seq, n3 = qkv.shape
    d = n3 // 3
    heads = d // SB_HEAD_DIM
    kern = functools.partial(_sb_attn_kernel, blk=blk)
    return pl.pallas_call(
        kern,
        out_shape=jax.ShapeDtypeStruct((seq, d), BF16),
        grid=(heads, seq // blk),
        in_specs=[
            pl.BlockSpec((blk, SB_HEAD_DIM), lambda h, i: (i, h)),
            pl.BlockSpec((seq, SB_HEAD_DIM), lambda h, i: (0, heads + h)),
            pl.BlockSpec((seq, SB_HEAD_DIM), lambda h, i: (0, 2 * heads + h)),
        ],
        out_specs=pl.BlockSpec((blk, SB_HEAD_DIM), lambda h, i: (i, h)),
        compiler_params=_cparams("parallel", "arbitrary"),
        name="sb_attention",
    )(qkv, qkv, qkv)


def _resid_matmul_kernel(x_ref, a_ref, w_ref, o_ref):
    o_ref[...] = x_ref[...] + jnp.dot(a_ref[...], w_ref[...], preferred_element_type=F32)


def resid_matmul(x, a, w, *, tm=512, name="resid_matmul"):
    m, n = x.shape
    k = a.shape[1]
    return pl.pallas_call(
        _resid_matmul_kernel,
        out_shape=jax.ShapeDtypeStruct((m, n), F32),
        grid=(m // tm,),
        in_specs=[
            pl.BlockSpec((tm, n), lambda i: (i, 0)),
            pl.BlockSpec((tm, k), lambda i: (i, 0)),
            _const_spec((k, n)),
        ],
        out_specs=pl.BlockSpec((tm, n), lambda i: (i, 0)),
        compiler_params=_cparams("parallel"),
        name=name,
    )(x, a, w)


def _ffn_kernel(x_ref, g_ref, wg_ref, wu_ref, wd_ref, o_ref, xn_ref, acc_ref):
    f = pl.program_id(1)

    @pl.when(f == 0)
    def _():
        _rms_to(xn_ref, x_ref, g_ref)
        acc_ref[...] = jnp.zeros_like(acc_ref)

    xn = xn_ref[...]
    gate = jnp.dot(xn, wg_ref[...], preferred_element_type=F32)
    up = jnp.dot(xn, wu_ref[...], preferred_element_type=F32)
    hid = (gate * _sigmoid(gate) * up).astype(BF16)
    acc_ref[...] += jnp.dot(hid, wd_ref[...], preferred_element_type=F32)

    @pl.when(f == pl.num_programs(1) - 1)
    def _():
        o_ref[...] = x_ref[...] + acc_ref[...]


def ffn(x, g, w_gate, w_up, w_down, *, tm=512, tf=512):
    m, d = x.shape
    dff = w_gate.shape[1]
    assert dff % tf == 0
    return pl.pallas_call(
        _ffn_kernel,
        out_shape=jax.ShapeDtypeStruct((m, d), F32),
        grid=(m // tm, dff // tf),
        in_specs=[
            pl.BlockSpec((tm, d), lambda i, f: (i, 0)),
            pl.BlockSpec((1, d), lambda i, f: (0, 0)),
            pl.BlockSpec((d, tf), lambda i, f: (0, f)),
            pl.BlockSpec((d, tf), lambda i, f: (0, f)),
            pl.BlockSpec((tf, d), lambda i, f: (f, 0)),
        ],
        out_specs=pl.BlockSpec((tm, d), lambda i, f: (i, 0)),
        scratch_shapes=[pltpu.VMEM((tm, d), BF16), pltpu.VMEM((tm, d), F32)],
        compiler_params=_cparams("parallel", "arbitrary"),
        name="ffn",
    )(x, g.reshape(1, d), w_gate, w_up, w_down)


def _ple_kernel(x_ref, g_ref, p_ref, wg_ref, wp_ref, o_ref, xn_ref, *, tn):
    _rms_to(xn_ref, x_ref, g_ref)
    xn = xn_ref[...]
    pb = p_ref[...].astype(BF16)
    for c in range(x_ref.shape[1] // tn):
        sl = slice(c * tn, (c + 1) * tn)
        gate = _sigmoid(jnp.dot(xn, wg_ref[:, sl], preferred_element_type=F32))
        proj = jnp.dot(pb, wp_ref[:, sl], preferred_element_type=F32)
        o_ref[:, sl] = x_ref[:, sl] + gate * proj


def ple(x, g, p, w_gate, w_proj, *, tm=512, tn=512):
    m, d = x.shape
    pd = p.shape[1]
    kern = functools.partial(_ple_kernel, tn=tn)
    return pl.pallas_call(
        kern,
        out_shape=jax.ShapeDtypeStruct((m, d), F32),
        grid=(m // tm,),
        in_specs=[
            pl.BlockSpec((tm, d), lambda i: (i, 0)),
            _const_spec((1, d)),
            pl.BlockSpec((tm, pd), lambda i: (i, 0)),
            _const_spec((d, d)),
            _const_spec((pd, d)),
        ],
        out_specs=pl.BlockSpec((tm, d), lambda i: (i, 0)),
        scratch_shapes=[pltpu.VMEM((tm, d), BF16)],
        compiler_params=_cparams("parallel"),
        name="ple",
    )(x, g.reshape(1, d), p, w_gate, w_proj)


def _s5_disc(lr, li, dt):
    mag = jnp.exp(lr * dt)
    ar = mag * jnp.cos(li * dt)
    ai = mag * jnp.sin(li * dt)
    den = lr * lr + li * li
    cr = ((ar - 1.0) * lr + ai * li) / den
    ci = (ai * lr - (ar - 1.0) * li) / den
    return ar, ai, cr, ci


def _s5_params_kernel(lr_ref, li_ref, ldt_ref, lrx_ref, lix_ref, ldtx_ref, bre_ref, bim_ref,
                      ar_ref, ai_ref, bbr_ref, bbi_ref):
    ar, ai, _, _ = _s5_disc(lr_ref[...], li_ref[...], jnp.exp(ldt_ref[...]))
    ar_ref[...] = ar
    ai_ref[...] = ai
    _, _, cr, ci = _s5_disc(lrx_ref[...], lix_ref[...], jnp.exp(ldtx_ref[...]))
    br = bre_ref[...]
    bi = bim_ref[...]
    bbr_ref[...] = cr * br - ci * bi
    bbi_ref[...] = cr * bi + ci * br


def s5_params(lam_re, lam_im, log_dt, b_re, b_im):
    g, p = lam_re.shape
    gs = b_re.shape[-1]
    ldt = jnp.broadcast_to(log_dt[:, None], (g, p))
    rep = lambda t: jnp.repeat(t, gs, axis=1)
    shp = jax.ShapeDtypeStruct
    return pl.pallas_call(
        _s5_params_kernel,
        out_shape=(shp((g, p), F32), shp((g, p), F32), shp((g, p * gs), F32), shp((g, p * gs), F32)),
        name="s5_params",
    )(lam_re, lam_im, ldt, rep(lam_re), rep(lam_im), rep(ldt),
      b_re.reshape(g, p * gs), b_im.reshape(g, p * gs))


S5_NBLK = 8
S5_T = 128
S5_PITCH = S5_T + 8


def _gelu_tanh(y):
    return 0.5 * y * (1.0 + jnp.tanh(math.sqrt(2.0 / math.pi) * (y + 0.044715 * (y * y * y))))


def _s5_scan_kernel(u_ref, bre_ref, bim_ref, cre_ref, cim_ref, ar_ref, ai_ref, d_ref, o_ref,
                    xr_ref, xi_ref, sr_ref, si_ref):
    nslab = xr_ref.shape[0]
    blk_states = nslab * LANES
    ucols = u_ref.shape[1] // S5_NBLK

    @pl.when(pl.program_id(0) == 0)
    def _():
        sr_ref[...] = jnp.zeros_like(sr_ref)
        si_ref[...] = jnp.zeros_like(si_ref)

    for kb in range(S5_NBLK):
        ukb = u_ref[:, kb * ucols:(kb + 1) * ucols].astype(BF16)
        bur = jnp.dot(ukb, bre_ref[kb], preferred_element_type=F32)
        bui = jnp.dot(ukb, bim_ref[kb], preferred_element_type=F32)
        for cs in range(nslab):
            xr_ref[cs, kb * S5_PITCH:kb * S5_PITCH + S5_T, :] = bur[:, cs * LANES:(cs + 1) * LANES]
            xi_ref[cs, kb * S5_PITCH:kb * S5_PITCH + S5_T, :] = bui[:, cs * LANES:(cs + 1) * LANES]

    def step(t, carry):
        new = []
        for cs in range(nslab):
            sr, si = carry[cs]
            ar = ar_ref[:, cs * LANES:(cs + 1) * LANES]
            ai = ai_ref[:, cs * LANES:(cs + 1) * LANES]
            rows = pl.ds(t, S5_NBLK, stride=S5_PITCH)
            nr = ar * sr - ai * si + xr_ref[cs, rows, :]
            ni = ar * si + ai * sr + xi_ref[cs, rows, :]
            xr_ref[cs, rows, :] = nr
            xi_ref[cs, rows, :] = ni
            new.append((nr, ni))
        return tuple(new)

    init = tuple((sr_ref[:, cs * LANES:(cs + 1) * LANES], si_ref[:, cs * LANES:(cs + 1) * LANES])
                 for cs in range(nslab))
    final = lax.fori_loop(0, S5_T, step, init)
    for cs in range(nslab):
        sr_ref[:, cs * LANES:(cs + 1) * LANES] = final[cs][0]
        si_ref[:, cs * LANES:(cs + 1) * LANES] = final[cs][1]

    for kb in range(S5_NBLK):
        rows = slice(kb * S5_PITCH, kb * S5_PITCH + S5_T)
        xr = jnp.concatenate([xr_ref[cs, rows, :] for cs in range(nslab)], axis=1).astype(BF16)
        xi = jnp.concatenate([xi_ref[cs, rows, :] for cs in range(nslab)], axis=1).astype(BF16)
        y = (jnp.dot(xr, cre_ref[kb], preferred_element_type=F32)
             - jnp.dot(xi, cim_ref[kb], preferred_element_type=F32))
        sl = slice(kb * ucols, (kb + 1) * ucols)
        y = y + d_ref[:, sl] * u_ref[:, sl]
        o_ref[:, sl] = _gelu_tanh(y)
    del blk_states


def s5_scan(u, bblk_re, bblk_im, cblk_re, cblk_im, a_re, a_im, d_skip):
    seq, d = u.shape
    nstate = a_re.size
    blk_states = nstate // S5_NBLK
    nslab = blk_states // LANES
    ucols = d // S5_NBLK
    return pl.pallas_call(
        _s5_scan_kernel,
        out_shape=jax.ShapeDtypeStruct((seq, d), F32),
        grid=(seq // S5_T,),
        in_specs=[
            pl.BlockSpec((S5_T, d), lambda c: (c, 0)),
            _const_spec((S5_NBLK, ucols, blk_states)),
            _const_spec((S5_NBLK, ucols, blk_states)),
            _const_spec((S5_NBLK, blk_states, ucols)),
            _const_spec((S5_NBLK, blk_states, ucols)),
            _const_spec((S5_NBLK, blk_states)),
            _const_spec((S5_NBLK, blk_states)),
            _const_spec((1, d)),
        ],
        out_specs=pl.BlockSpec((S5_T, d), lambda c: (c, 0)),
        scratch_shapes=[
            pltpu.VMEM((nslab, S5_NBLK * S5_PITCH, LANES), F32),
            pltpu.VMEM((nslab, S5_NBLK * S5_PITCH, LANES), F32),
            pltpu.VMEM((S5_NBLK, blk_states), F32),
            pltpu.VMEM((S5_NBLK, blk_states), F32),
        ],
        compiler_params=_cparams("arbitrary"),
        name="s5_scan",
    )(u, bblk_re, bblk_im, cblk_re, cblk_im,
      a_re.reshape(S5_NBLK, blk_states), a_im.reshape(S5_NBLK, blk_states), d_skip.reshape(1, d))


def _glu_out_kernel(x_ref, g_ref, wglu_ref, wout_ref, o_ref):
    g = g_ref[...]
    gate = _sigmoid(jnp.dot(g.astype(BF16), wglu_ref[...], preferred_element_type=F32))
    h = (g * gate).astype(BF16)
    o_ref[...] = x_ref[...] + jnp.dot(h, wout_ref[...], preferred_element_type=F32)


def glu_out(x, g, w_glu, w_out, *, tm=256):
    m, d = x.shape
    return pl.pallas_call(
        _glu_out_kernel,
        out_shape=jax.ShapeDtypeStruct((m, d), F32),
        grid=(m // tm,),
        in_specs=[
            pl.BlockSpec((tm, d), lambda i: (i, 0)),
            pl.BlockSpec((tm, d), lambda i: (i, 0)),
            _const_spec((d, d)),
            _const_spec((d, d)),
        ],
        out_specs=pl.BlockSpec((tm, d), lambda i: (i, 0)),
        compiler_params=_cparams("parallel"),
        name="s5_glu_out",
    )(x, g, w_glu, w_out)


ML_T = 128


def _mlstm_kernel(q_ref, k_ref, v_ref, o_ref, gc_ref, gr_ref, bc_ref, br_ref, hg_ref, out_ref,
                  c_ref, n_ref, m_ref):
    t = ML_T
    nh = ML_HEADS
    dqk = q_ref.shape[1] // nh
    dv = v_ref.shape[1] // nh
    kscale = dqk ** -0.5

    @pl.when(pl.program_id(0) == 0)
    def _():
        c_ref[...] = jnp.zeros_like(c_ref)
        n_ref[...] = jnp.zeros_like(n_ref)
        m_ref[...] = jnp.zeros_like(m_ref)

    row = lax.broadcasted_iota(jnp.int32, (t, t), 0)
    col = lax.broadcasted_iota(jnp.int32, (t, t), 1)
    causal = col <= row
    lower = causal.astype(F32)
    upper = (row <= col).astype(F32)

    gcol = gc_ref[...] + bc_ref[...]
    lf_col = _log_sigmoid(gcol)
    bcum_col = jnp.dot(lower, lf_col, preferred_element_type=F32, precision=lax.Precision.HIGHEST)
    grow = gr_ref[...] + br_ref[...]
    lf_row = _log_sigmoid(grow)
    bcum_row = jnp.dot(lf_row, upper, preferred_element_type=F32, precision=lax.Precision.HIGHEST)

    for h in range(nh):
        qh = q_ref[:, h * dqk:(h + 1) * dqk]
        kh = k_ref[:, h * dqk:(h + 1) * dqk]
        vh = v_ref[:, h * dv:(h + 1) * dv]
        bc = bcum_col[:, nh + h:nh + h + 1]
        ic = gcol[:, h:h + 1]
        brw = bcum_row[nh + h:nh + h + 1, :]
        ir = grow[h:h + 1, :]
        m_st = m_ref[h:h + 1, 0:1]
        c_st = c_ref[h]
        n_st = n_ref[h:h + 1, :]

        dmat = jnp.where(causal, bc - brw + ir, -jnp.inf)
        inter = bc + m_st
        m_row = jnp.maximum(jnp.max(dmat, axis=1, keepdims=True), inter)
        w = jnp.exp(dmat - m_row)
        w_inter = jnp.exp(inter - m_row)
        s = lax.dot_general(qh, kh, (((1,), (1,)), ((), ())), preferred_element_type=F32) * (w * kscale)
        num = (jnp.dot(s.astype(BF16), vh, preferred_element_type=F32)
               + w_inter * jnp.dot(qh, c_st.astype(BF16), preferred_element_type=F32))
        qn = jnp.sum(qh.astype(F32) * n_st, axis=1, keepdims=True)
        den = jnp.sum(s, axis=1, keepdims=True) + w_inter * qn
        h_t = num / jnp.maximum(jnp.abs(den), jnp.exp(-m_row))

        b_last = bc[t - 1:t, :]
        dec_row = b_last - brw + ir
        m_new = jnp.maximum(b_last + m_st, jnp.max(dec_row, axis=1, keepdims=True))
        dec_col = b_last - bc + ic
        wk = jnp.exp(dec_col - m_new) * kscale
        carry_scale = jnp.exp(b_last + m_st - m_new)
        kw = kh.astype(F32) * wk
        c_ref[h] = carry_scale * c_st + lax.dot_general(
            kw.astype(BF16), vh, (((0,), (0,)), ((), ())), preferred_element_type=F32)
        n_ref[h:h + 1, :] = carry_scale * n_st + jnp.sum(kw, axis=0, keepdims=True)
        m_ref[h:h + 1, :] = jnp.broadcast_to(m_new, (1, m_ref.shape[1]))

        ms = jnp.mean(h_t * h_t, axis=1, keepdims=True)
        hn = h_t * lax.rsqrt(ms + EPS) * hg_ref[:, h * dv:(h + 1) * dv]
        og = _sigmoid(o_ref[:, h * dv:(h + 1) * dv])
        out_ref[:, h * dv:(h + 1) * dv] = (og * hn).astype(out_ref.dtype)


def mlstm_scan(qkv, o, g_col, g_row, bias, head_gain):
    seq = qkv.shape[0]
    d = o.shape[1]
    nqk = (qkv.shape[1] - d) // 2
    nh = ML_HEADS
    assert d == 2 * nqk
    bias_col = jnp.zeros((1, LANES), F32).at[0, :2 * nh].set(bias)
    bias_row = bias.reshape(2 * nh, 1)
    return pl.pallas_call(
        _mlstm_kernel,
        out_shape=jax.ShapeDtypeStruct((seq, d), BF16),
        grid=(seq // ML_T,),
        in_specs=[
            pl.BlockSpec((ML_T, nqk), lambda c: (c, 0)),
            pl.BlockSpec((ML_T, nqk), lambda c: (c, 1)),
            pl.BlockSpec((ML_T, d), lambda c: (c, 1)),
            pl.BlockSpec((ML_T, d), lambda c: (c, 0)),
            pl.BlockSpec((ML_T, LANES), lambda c: (c, 0)),
            pl.BlockSpec((2 * nh, ML_T), lambda c: (0, c)),
            _const_spec((1, LANES)),
            _const_spec((2 * nh, 1)),
            _const_spec((1, d)),
        ],
        out_specs=pl.BlockSpec((ML_T, d), lambda c: (c, 0)),
        scratch_shapes=[
            pltpu.VMEM((nh, nqk // nh, d // nh), F32),
            pltpu.VMEM((nh, nqk // nh), F32),
            pltpu.VMEM((nh, LANES), F32),
        ],
        compiler_params=_cparams("arbitrary"),
        name="mlstm_scan",
    )(qkv, qkv, qkv, o, g_col, g_row, bias_col, bias_row, head_gain.reshape(1, d))


def _sb_layer(x, g, w_in, q_gain, k_gain, w_out):
    qkv = sb_inproj(x, g, w_in.astype(BF16), q_gain, k_gain)
    o = sb_attention(qkv)
    return resid_matmul(x, o, w_out.astype(BF16), name="sb_out")


def _block_diag(t):
    nb, gb, r, c = t.shape
    eye = jnp.eye(gb, dtype=t.dtype)
    return (t[:, :, :, None, :] * eye[None, :, None, :, None]).reshape(nb, gb * r, gb * c)


def _s5_layer(x, g, w_in, lam_re, lam_im, log_dt, b_re, b_im, c_re, c_im, d_skip, w_glu, w_out):
    ngrp, nst = lam_re.shape
    gs = b_re.shape[-1]
    gb = ngrp // S5_NBLK
    u = norm_matmul(x, g, w_in.astype(BF16), F32, name="s5_in")
    a_re, a_im, bb_re, bb_im = s5_params(lam_re, lam_im, log_dt, b_re, b_im)
    to_b = lambda t: _block_diag(t.reshape(S5_NBLK, gb, nst, gs).transpose(0, 1, 3, 2)).astype(BF16)
    to_c = lambda t: _block_diag(t.reshape(S5_NBLK, gb, gs, nst).transpose(0, 1, 3, 2)).astype(BF16)
    y = s5_scan(u, to_b(bb_re), to_b(bb_im), to_c(c_re), to_c(c_im), a_re, a_im, d_skip)
    return glu_out(x, y, w_glu.astype(BF16), w_out.astype(BF16))


def _mlstm_layer(x, g, w_in, gate_bias, head_gain, w_out):
    d = x.shape[1]
    nh = ML_HEADS
    nqk = d // 2
    w_qkv = w_in[:, :2 * nqk + d].astype(BF16)
    w_o = w_in[:, 2 * nqk + d:2 * nqk + 2 * d].astype(BF16)
    w_g = jnp.pad(w_in[:, 2 * nqk + 2 * d:], ((0, 0), (0, LANES - 2 * nh))).astype(BF16)
    qkv = norm_matmul(x, g, w_qkv, BF16, name="ml_in_qkv")
    o = norm_matmul(x, g, w_o, F32, name="ml_in_o")
    g_col = norm_matmul(x, g, w_g, F32, name="ml_in_gates")
    g_row = g_col[:, :2 * nh].T
    hs = mlstm_scan(qkv, o, g_col, g_row, gate_bias, head_gain)
    return resid_matmul(x, hs, w_out.astype(BF16), name="ml_out")


def kernel(x, p, norm_mix, norm_ffn, sb_w_in, sb_q_gain, sb_k_gain, sb_w_out, s5_w_in, s5_lam_re, s5_lam_im, s5_log_dt, s5_b_re, s5_b_im, s5_c_re, s5_c_im, s5_d, s5_w_glu, s5_w_out, ml_w_in, ml_gate_bias, ml_head_gain, ml_w_out, ffn_w_gate, ffn_w_up, ffn_w_down, ple_norm, ple_w_gate, ple_w_proj):
    bsz, seq, d = x.shape
    depth = norm_mix.shape[0]
    outs = []
    for b in range(bsz):
        h = x[b]
        for i in range(depth):
            kind, j = i % 3, i // 3
            if kind == 0:
                h = _sb_layer(h, norm_mix[i], sb_w_in[j], sb_q_gain[j], sb_k_gain[j], sb_w_out[j])
            elif kind == 1:
                h = _s5_layer(h, norm_mix[i], s5_w_in[j], s5_lam_re[j], s5_lam_im[j], s5_log_dt[j],
                              s5_b_re[j], s5_b_im[j], s5_c_re[j], s5_c_im[j], s5_d[j],
                              s5_w_glu[j], s5_w_out[j])
            else:
                h = _mlstm_layer(h, norm_mix[i], ml_w_in[j], ml_gate_bias[j], ml_head_gain[j],
                                 ml_w_out[j])
            h = ffn(h, norm_ffn[i], ffn_w_gate[i].astype(BF16), ffn_w_up[i].astype(BF16),
                    ffn_w_down[i].astype(BF16))
            h = ple(h, ple_norm[i], p[i, b], ple_w_gate[i].astype(BF16), ple_w_proj[i].astype(BF16))
        outs.append(h)
    return jnp.stack(outs)
```

```python
import functools
import math

import jax
import jax.numpy as jnp
from jax import lax
from jax.experimental import pallas as pl
from jax.experimental.pallas import tpu as pltpu

F32 = jnp.float32
BF16 = jnp.bfloat16

EPS = 1e-6
LANES = 128
SUBLANES = 8
VMEM_LIMIT = 56 * 1024 * 1024
COL_CHUNK = 512

SB_HEAD_DIM = 128
ML_HEADS = 8
EXP_ZERO_BELOW = -110.0


def _cparams(*sem):
    return pltpu.CompilerParams(dimension_semantics=sem, vmem_limit_bytes=VMEM_LIMIT)


def _const_spec(shape):
    nd = len(shape)
    return pl.BlockSpec(shape, lambda *_: (0,) * nd, pipeline_mode=pl.Buffered(1))


def _rms_to(dst_ref, x_ref, g_ref, copy_ref=None, chunk=256):
    rows = x_ref.shape[0]
    chunk = min(chunk, rows)

    def body(c, _):
        r = pl.multiple_of(c * chunk, chunk)
        x = x_ref[pl.ds(r, chunk), :]
        ms = jnp.mean(x * x, axis=-1, keepdims=True)
        dst_ref[pl.ds(r, chunk), :] = (x * lax.rsqrt(ms + EPS) * g_ref[...]).astype(dst_ref.dtype)
        if copy_ref is not None:
            copy_ref[pl.ds(r, chunk), :] = x
        return 0

    lax.fori_loop(0, rows // chunk, body, 0)


def _col_chunks(n):
    c = min(COL_CHUNK, n)
    assert n % c == 0
    return [slice(i * c, (i + 1) * c) for i in range(n // c)]


def _log_sigmoid(z):
    return jnp.minimum(z, 0.0) - jnp.log1p(jnp.exp(-jnp.abs(z)))


def _sigmoid(z):
    return 1.0 / (1.0 + jnp.exp(-z))


def _norm_matmul_kernel(x_ref, g_ref, w_ref, o_ref, xn_ref):
    @pl.when(pl.program_id(1) == 0)
    def _():
        _rms_to(xn_ref, x_ref, g_ref)

    xn = xn_ref[...]
    for sl in _col_chunks(o_ref.shape[1]):
        o_ref[:, sl] = jnp.dot(xn, w_ref[:, sl], preferred_element_type=F32).astype(o_ref.dtype)


def norm_matmul(x, g, w, out_dtype, *, tm, tn, name):
    m, k = x.shape
    n = w.shape[1]
    assert m % tm == 0 and n % tn == 0
    return pl.pallas_call(
        _norm_matmul_kernel,
        out_shape=jax.ShapeDtypeStruct((m, n), out_dtype),
        grid=(m // tm, n // tn),
        in_specs=[
            pl.BlockSpec((tm, k), lambda i, j: (i, 0)),
            pl.BlockSpec((1, k), lambda i, j: (0, 0)),
            pl.BlockSpec((k, tn), lambda i, j: (0, j)),
        ],
        out_specs=pl.BlockSpec((tm, tn), lambda i, j: (i, j)),
        scratch_shapes=[pltpu.VMEM((tm, k), BF16)],
        compiler_params=_cparams("parallel", "arbitrary"),
        name=name,
    )(x, g.reshape(1, k), w)


def _sb_inproj_kernel(x_ref, g_ref, w_ref, hg_ref, o_ref, xn_ref, *, n_qk_tiles, n_q_tiles):
    j = pl.program_id(1)

    @pl.when(j == 0)
    def _():
        _rms_to(xn_ref, x_ref, g_ref)

    y = jnp.dot(xn_ref[...], w_ref[...], preferred_element_type=F32)
    tn = y.shape[1]

    @pl.when(j < n_qk_tiles)
    def _():
        scale = jnp.where(j < n_q_tiles, SB_HEAD_DIM ** -0.5, 1.0).astype(F32)
        for h in range(tn // SB_HEAD_DIM):
            blk = y[:, h * SB_HEAD_DIM:(h + 1) * SB_HEAD_DIM]
            ms = jnp.mean(blk * blk, axis=-1, keepdims=True)
            nb = blk * lax.rsqrt(ms + EPS) * hg_ref[...]
            o_ref[:, h * SB_HEAD_DIM:(h + 1) * SB_HEAD_DIM] = (nb * scale).astype(o_ref.dtype)

    @pl.when(j >= n_qk_tiles)
    def _():
        o_ref[...] = y.astype(o_ref.dtype)


def sb_inproj(x, g, w, q_gain, k_gain, *, tm=1024, tn=512):
    m, k = x.shape
    n = w.shape[1]
    d = n // 3
    n_q_tiles = d // tn
    hg = jnp.stack([q_gain, k_gain]).reshape(2, 1, SB_HEAD_DIM)
    kern = functools.partial(_sb_inproj_kernel, n_qk_tiles=2 * n_q_tiles, n_q_tiles=n_q_tiles)
    return pl.pallas_call(
        kern,
        out_shape=jax.ShapeDtypeStruct((m, n), BF16),
        grid=(m // tm, n // tn),
        in_specs=[
            pl.BlockSpec((tm, k), lambda i, j: (i, 0)),
            pl.BlockSpec((1, k), lambda i, j: (0, 0)),
            pl.BlockSpec((k, tn), lambda i, j: (0, j)),
            pl.BlockSpec((None, 1, SB_HEAD_DIM), lambda i, j: (jnp.minimum(j // n_q_tiles, 1), 0, 0)),
        ],
        out_specs=pl.BlockSpec((tm, tn), lambda i, j: (i, j)),
        scratch_shapes=[pltpu.VMEM((tm, k), BF16)],
        compiler_params=_cparams("parallel", "arbitrary"),
        name="sb_inproj",
    )(x, g.reshape(1, k), w, hg)


def _sb_attn_kernel(q_ref, k_ref, v_ref, suf_ref, o_ref, *, blk, nsub):
    qi = pl.program_id(1)
    row = lax.broadcasted_iota(jnp.int32, (blk, blk), 0)
    col = lax.broadcasted_iota(jnp.int32, (blk, blk), 1)
    strict = col < row
    suffix = suf_ref[...]

    def suffix_sum(lg):
        hi = lg.astype(BF16)
        lo = (lg - hi.astype(F32)).astype(BF16)
        return (jnp.dot(hi, suffix, preferred_element_type=F32)
                + jnp.dot(lo, suffix, preferred_element_type=F32))

    def scores(q, kb):
        r = pl.multiple_of(kb * blk, blk)
        k = k_ref[pl.ds(r, blk), :]
        z = lax.dot_general(q, k, (((1,), (1,)), ((), ())), preferred_element_type=F32)
        return z, v_ref[pl.ds(r, blk), :]

    def own_block(q, gb):
        z, v = scores(q, gb)
        cs = suffix_sum(jnp.where(strict, _log_sigmoid(-z), 0.0))
        a = jnp.where(strict, jnp.exp(z + cs), 0.0)
        return cs[:, 0:1], jnp.dot(a.astype(BF16), v, preferred_element_type=F32)

    def earlier_block(q, kb, carry, valid):
        z, v = scores(q, kb)
        cs = suffix_sum(_log_sigmoid(-z))
        a = jnp.where(valid, jnp.exp(z + cs + carry), 0.0)
        carry = carry + jnp.where(valid, cs[:, 0:1], 0.0)
        return carry, jnp.dot(a.astype(BF16), v, preferred_element_type=F32)

    qs = [q_ref[j * blk:(j + 1) * blk, :] for j in range(nsub)]
    carries, accs = [], []
    for j in range(nsub):
        gb = qi * nsub + j
        c_own, acc_own = own_block(qs[j], gb)
        carry, acc_prev = earlier_block(qs[j], jnp.maximum(gb - 1, 0), c_own, gb >= 1)
        carries.append(carry)
        accs.append(acc_own + acc_prev)

    n_more = qi * nsub + nsub - 2

    def cond(state):
        i, carries, _ = state
        live = carries[0]
        for c in carries[1:]:
            live = jnp.maximum(live, c)
        return jnp.logical_and(i < n_more, jnp.max(live) > EXP_ZERO_BELOW)

    def body(state):
        i, carries, accs = state
        new_c, new_a = [], []
        for j in range(nsub):
            kb = qi * nsub + j - 2 - i
            c, a = earlier_block(qs[j], jnp.maximum(kb, 0), carries[j], kb >= 0)
            new_c.append(c)
            new_a.append(accs[j] + a)
        return i + 1, tuple(new_c), tuple(new_a)

    _, _, accs = lax.while_loop(cond, body, (jnp.int32(0), tuple(carries), tuple(accs)))
    for j in range(nsub):
        o_ref[j * blk:(j + 1) * blk, :] = accs[j].astype(o_ref.dtype)


def sb_attention(qkv, *, blk=256, nsub=4):
    seq, n3 = qkv.shape
    d = n3 // 3
    heads = d // SB_HEAD_DIM
    tq = blk * nsub
    assert seq % tq == 0
    idx = jnp.arange(blk)
    suffix = (idx[:, None] >= idx[None, :]).astype(BF16)
    kern = functools.partial(_sb_attn_kernel, blk=blk, nsub=nsub)
    return pl.pallas_call(
        kern,
        out_shape=jax.ShapeDtypeStruct((seq, d), BF16),
        grid=(heads, seq // tq),
        in_specs=[
            pl.BlockSpec((tq, SB_HEAD_DIM), lambda h, i: (i, h)),
            pl.BlockSpec((seq, SB_HEAD_DIM), lambda h, i: (0, heads + h)),
            pl.BlockSpec((seq, SB_HEAD_DIM), lambda h, i: (0, 2 * heads + h)),
            _const_spec((blk, blk)),
        ],
        out_specs=pl.BlockSpec((tq, SB_HEAD_DIM), lambda h, i: (i, h)),
        compiler_params=_cparams("parallel", "arbitrary"),
        name="sb_attention",
    )(qkv, qkv, qkv, suffix)


def _resid_matmul_kernel(x_ref, a_ref, w_ref, o_ref):
    a = a_ref[...]
    for sl in _col_chunks(o_ref.shape[1]):
        o_ref[:, sl] = x_ref[:, sl] + jnp.dot(a, w_ref[:, sl], preferred_element_type=F32)


def resid_matmul(x, a, w, *, tm=512, name="resid_matmul"):
    m, n = x.shape
    k = a.shape[1]
    return pl.pallas_call(
        _resid_matmul_kernel,
        out_shape=jax.ShapeDtypeStruct((m, n), F32),
        grid=(m // tm,),
        in_specs=[
            pl.BlockSpec((tm, n), lambda i: (i, 0)),
            pl.BlockSpec((tm, k), lambda i: (i, 0)),
            _const_spec((k, n)),
        ],
        out_specs=pl.BlockSpec((tm, n), lambda i: (i, 0)),
        compiler_params=_cparams("parallel"),
        name=name,
    )(x, a, w)


def _ffn_kernel(x_ref, g_ref, wg_ref, wu_ref, wd_ref, o_ref, xn_ref):
    @pl.when(pl.program_id(1) == 0)
    def _():
        _rms_to(xn_ref, x_ref, g_ref, copy_ref=o_ref)

    xn = xn_ref[...]
    gate = jnp.dot(xn, wg_ref[...], preferred_element_type=F32)
    up = jnp.dot(xn, wu_ref[...], preferred_element_type=F32)
    hid = (gate * _sigmoid(gate) * up).astype(BF16)
    o_ref[...] += jnp.dot(hid, wd_ref[...], preferred_element_type=F32)


def ffn(x, g, w_gate, w_up, w_down, *, tm=1024, tf=512):
    m, d = x.shape
    dff = w_gate.shape[1]
    assert m % tm == 0 and dff % tf == 0
    return pl.pallas_call(
        _ffn_kernel,
        out_shape=jax.ShapeDtypeStruct((m, d), F32),
        grid=(m // tm, dff // tf),
        in_specs=[
            pl.BlockSpec((tm, d), lambda i, f: (i, 0)),
            pl.BlockSpec((1, d), lambda i, f: (0, 0)),
            pl.BlockSpec((d, tf), lambda i, f: (0, f)),
            pl.BlockSpec((d, tf), lambda i, f: (0, f)),
            pl.BlockSpec((tf, d), lambda i, f: (f, 0)),
        ],
        out_specs=pl.BlockSpec((tm, d), lambda i, f: (i, 0)),
        scratch_shapes=[pltpu.VMEM((tm, d), BF16)],
        compiler_params=_cparams("parallel", "arbitrary"),
        name="ffn",
    )(x, g.reshape(1, d), w_gate, w_up, w_down)


def _ple_kernel(x_ref, g_ref, p_ref, wg_ref, wp_ref, o_ref, xn_ref):
    _rms_to(xn_ref, x_ref, g_ref)
    xn = xn_ref[...]
    pb = p_ref[...].astype(BF16)
    for sl in _col_chunks(o_ref.shape[1]):
        gate = _sigmoid(jnp.dot(xn, wg_ref[:, sl], preferred_element_type=F32))
        proj = jnp.dot(pb, wp_ref[:, sl], preferred_element_type=F32)
        o_ref[:, sl] = x_ref[:, sl] + gate * proj


def ple(x, g, p, w_gate, w_proj, *, tm=512):
    m, d = x.shape
    pd = p.shape[1]
    return pl.pallas_call(
        _ple_kernel,
        out_shape=jax.ShapeDtypeStruct((m, d), F32),
        grid=(m // tm,),
        in_specs=[
            pl.BlockSpec((tm, d), lambda i: (i, 0)),
            _const_spec((1, d)),
            pl.BlockSpec((tm, pd), lambda i: (i, 0)),
            _const_spec((d, d)),
            _const_spec((pd, d)),
        ],
        out_specs=pl.BlockSpec((tm, d), lambda i: (i, 0)),
        scratch_shapes=[pltpu.VMEM((tm, d), BF16)],
        compiler_params=_cparams("parallel"),
        name="ple",
    )(x, g.reshape(1, d), p, w_gate, w_proj)


def _s5_disc(lr, li, dt):
    mag = jnp.exp(lr * dt)
    ar = mag * jnp.cos(li * dt)
    ai = mag * jnp.sin(li * dt)
    den = lr * lr + li * li
    cr = ((ar - 1.0) * lr + ai * li) / den
    ci = (ai * lr - (ar - 1.0) * li) / den
    return ar, ai, cr, ci


def _s5_params_kernel(lr_ref, li_ref, ldt_ref, lrx_ref, lix_ref, ldtx_ref, bre_ref, bim_ref,
                      ar_ref, ai_ref, bbr_ref, bbi_ref):
    ar, ai, _, _ = _s5_disc(lr_ref[...], li_ref[...], jnp.exp(ldt_ref[...]))
    ar_ref[...] = ar
    ai_ref[...] = ai
    _, _, cr, ci = _s5_disc(lrx_ref[...], lix_ref[...], jnp.exp(ldtx_ref[...]))
    br = bre_ref[...]
    bi = bim_ref[...]
    bbr_ref[...] = cr * br - ci * bi
    bbi_ref[...] = cr * bi + ci * br


def s5_params(lam_re, lam_im, log_dt, b_re, b_im):
    g, p = lam_re.shape
    gs = b_re.shape[-1]
    ldt = jnp.broadcast_to(log_dt[:, None], (g, p))
    rep = lambda t: jnp.repeat(t, gs, axis=1)
    shp = jax.ShapeDtypeStruct
    return pl.pallas_call(
        _s5_params_kernel,
        out_shape=(shp((g, p), F32), shp((g, p), F32), shp((g, p * gs), F32), shp((g, p * gs), F32)),
        name="s5_params",
    )(lam_re, lam_im, ldt, rep(lam_re), rep(lam_im), rep(ldt),
      b_re.reshape(g, p * gs), b_im.reshape(g, p * gs))


S5_NBLK = 8
S5_T = 128
S5_PITCH = S5_T + 8


def _gelu_tanh(y):
    return 0.5 * y * (1.0 + jnp.tanh(math.sqrt(2.0 / math.pi) * (y + 0.044715 * (y * y * y))))


def _s5_scan_kernel(u_ref, bre_ref, bim_ref, cre_ref, cim_ref, ar_ref, ai_ref, d_ref, o_ref,
                    xr_ref, xi_ref, sr_ref, si_ref):
    nslab = xr_ref.shape[0]
    ucols = u_ref.shape[1] // S5_NBLK

    @pl.when(pl.program_id(0) == 0)
    def _():
        sr_ref[...] = jnp.zeros_like(sr_ref)
        si_ref[...] = jnp.zeros_like(si_ref)

    for kb in range(S5_NBLK):
        ukb = u_ref[:, kb * ucols:(kb + 1) * ucols].astype(BF16)
        bur = jnp.dot(ukb, bre_ref[kb], preferred_element_type=F32)
        bui = jnp.dot(ukb, bim_ref[kb], preferred_element_type=F32)
        for cs in range(nslab):
            xr_ref[cs, kb * S5_PITCH:kb * S5_PITCH + S5_T, :] = bur[:, cs * LANES:(cs + 1) * LANES]
            xi_ref[cs, kb * S5_PITCH:kb * S5_PITCH + S5_T, :] = bui[:, cs * LANES:(cs + 1) * LANES]

    def step(t, carry):
        new = []
        for cs in range(nslab):
            sr, si = carry[cs]
            ar = ar_ref[:, cs * LANES:(cs + 1) * LANES]
            ai = ai_ref[:, cs * LANES:(cs + 1) * LANES]
            rows = pl.ds(t, S5_NBLK, stride=S5_PITCH)
            nr = ar * sr - ai * si + xr_ref[cs, rows, :]
            ni = ar * si + ai * sr + xi_ref[cs, rows, :]
            xr_ref[cs, rows, :] = nr
            xi_ref[cs, rows, :] = ni
            new.append((nr, ni))
        return tuple(new)

    init = tuple((sr_ref[:, cs * LANES:(cs + 1) * LANES], si_ref[:, cs * LANES:(cs + 1) * LANES])
                 for cs in range(nslab))
    final = lax.fori_loop(0, S5_T, step, init, unroll=2)
    for cs in range(nslab):
        sr_ref[:, cs * LANES:(cs + 1) * LANES] = final[cs][0]
        si_ref[:, cs * LANES:(cs + 1) * LANES] = final[cs][1]

    for kb in range(S5_NBLK):
        rows = slice(kb * S5_PITCH, kb * S5_PITCH + S5_T)
        xr = jnp.concatenate([xr_ref[cs, rows, :] for cs in range(nslab)], axis=1).astype(BF16)
        xi = jnp.concatenate([xi_ref[cs, rows, :] for cs in range(nslab)], axis=1).astype(BF16)
        y = (jnp.dot(xr, cre_ref[kb], preferred_element_type=F32)
             - jnp.dot(xi, cim_ref[kb], preferred_element_type=F32))
        sl = slice(kb * ucols, (kb + 1) * ucols)
        y = y + d_ref[:, sl] * u_ref[:, sl]
        o_ref[:, sl] = _gelu_tanh(y)


def s5_scan(u, bblk_re, bblk_im, cblk_re, cblk_im, a_re, a_im, d_skip):
    seq, d = u.shape
    nstate = a_re.size
    blk_states = nstate // S5_NBLK
    nslab = blk_states // LANES
    ucols = d // S5_NBLK
    return pl.pallas_call(
        _s5_scan_kernel,
        out_shape=jax.ShapeDtypeStruct((seq, d), F32),
        grid=(seq // S5_T,),
        in_specs=[
            pl.BlockSpec((S5_T, d), lambda c: (c, 0)),
            _const_spec((S5_NBLK, ucols, blk_states)),
            _const_spec((S5_NBLK, ucols, blk_states)),
            _const_spec((S5_NBLK, blk_states, ucols)),
            _const_spec((S5_NBLK, blk_states, ucols)),
            _const_spec((S5_NBLK, blk_states)),
            _const_spec((S5_NBLK, blk_states)),
            _const_spec((1, d)),
        ],
        out_specs=pl.BlockSpec((S5_T, d), lambda c: (c, 0)),
        scratch_shapes=[
            pltpu.VMEM((nslab, S5_NBLK * S5_PITCH, LANES), F32),
            pltpu.VMEM((nslab, S5_NBLK * S5_PITCH, LANES), F32),
            pltpu.VMEM((S5_NBLK, blk_states), F32),
            pltpu.VMEM((S5_NBLK, blk_states), F32),
        ],
        compiler_params=_cparams("arbitrary"),
        name="s5_scan",
    )(u, bblk_re, bblk_im, cblk_re, cblk_im,
      a_re.reshape(S5_NBLK, blk_states), a_im.reshape(S5_NBLK, blk_states), d_skip.reshape(1, d))


def _glu_out_kernel(x_ref, g_ref, wglu_ref, wout_ref, o_ref, h_ref):
    gb = g_ref[...].astype(BF16)
    chunks = _col_chunks(o_ref.shape[1])
    for sl in chunks:
        gate = _sigmoid(jnp.dot(gb, wglu_ref[:, sl], preferred_element_type=F32))
        h_ref[:, sl] = (g_ref[:, sl] * gate).astype(BF16)
    h = h_ref[...]
    for sl in chunks:
        o_ref[:, sl] = x_ref[:, sl] + jnp.dot(h, wout_ref[:, sl], preferred_element_type=F32)


def glu_out(x, g, w_glu, w_out, *, tm=512):
    m, d = x.shape
    return pl.pallas_call(
        _glu_out_kernel,
        out_shape=jax.ShapeDtypeStruct((m, d), F32),
        grid=(m // tm,),
        in_specs=[
            pl.BlockSpec((tm, d), lambda i: (i, 0)),
            pl.BlockSpec((tm, d), lambda i: (i, 0)),
            _const_spec((d, d)),
            _const_spec((d, d)),
        ],
        out_specs=pl.BlockSpec((tm, d), lambda i: (i, 0)),
        scratch_shapes=[pltpu.VMEM((tm, d), BF16)],
        compiler_params=_cparams("parallel"),
        name="s5_glu_out",
    )(x, g, w_glu, w_out)


def _ml_inproj_kernel(x_ref, g_ref, w_ref, wg_ref, qkv_ref, o_ref, gates_ref, xn_ref, *, n_qkv_tiles):
    j = pl.program_id(1)

    @pl.when(j == 0)
    def _():
        _rms_to(xn_ref, x_ref, g_ref)
        gates_ref[...] = jnp.dot(xn_ref[...], wg_ref[...], preferred_element_type=F32)

    y = jnp.dot(xn_ref[...], w_ref[...], preferred_element_type=F32)

    @pl.when(j < n_qkv_tiles)
    def _():
        qkv_ref[...] = y.astype(qkv_ref.dtype)

    @pl.when(j >= n_qkv_tiles)
    def _():
        o_ref[...] = y


def ml_inproj(x, g, w, w_gates, n_qkv, n_o, *, tm=1024, tn=512):
    m, k = x.shape
    tq, to = n_qkv // tn, n_o // tn
    kern = functools.partial(_ml_inproj_kernel, n_qkv_tiles=tq)
    shp = jax.ShapeDtypeStruct
    return pl.pallas_call(
        kern,
        out_shape=(shp((m, n_qkv), BF16), shp((m, n_o), F32), shp((m, LANES), F32)),
        grid=(m // tm, tq + to),
        in_specs=[
            pl.BlockSpec((tm, k), lambda i, j: (i, 0)),
            pl.BlockSpec((1, k), lambda i, j: (0, 0)),
            pl.BlockSpec((k, tn), lambda i, j: (0, j)),
            _const_spec((k, LANES)),
        ],
        out_specs=(
            pl.BlockSpec((tm, tn), lambda i, j: (i, jnp.minimum(j, tq - 1))),
            pl.BlockSpec((tm, tn), lambda i, j: (i, jnp.maximum(j - tq, 0))),
            pl.BlockSpec((tm, LANES), lambda i, j: (i, 0)),
        ),
        scratch_shapes=[pltpu.VMEM((tm, k), BF16)],
        compiler_params=_cparams("parallel", "arbitrary"),
        name="ml_inproj",
    )(x, g.reshape(1, k), w, w_gates)


ML_T = 128


def _mlstm_kernel(q_ref, k_ref, v_ref, o_ref, gc_ref, gr_ref, bc_ref, br_ref, hg_ref, out_ref,
                  c_ref, n_ref, m_ref):
    t = ML_T
    nh = ML_HEADS
    dqk = q_ref.shape[1] // nh
    dv = v_ref.shape[1] // nh
    kscale = dqk ** -0.5

    @pl.when(pl.program_id(0) == 0)
    def _():
        c_ref[...] = jnp.zeros_like(c_ref)
        n_ref[...] = jnp.zeros_like(n_ref)
        m_ref[...] = jnp.zeros_like(m_ref)

    row = lax.broadcasted_iota(jnp.int32, (t, t), 0)
    col = lax.broadcasted_iota(jnp.int32, (t, t), 1)
    causal = col <= row
    lower = causal.astype(F32)
    upper = (row <= col).astype(F32)

    gcol = gc_ref[...] + bc_ref[...]
    lf_col = _log_sigmoid(gcol)
    bcum_col = jnp.dot(lower, lf_col, preferred_element_type=F32, precision=lax.Precision.HIGHEST)
    grow = gr_ref[...] + br_ref[...]
    lf_row = _log_sigmoid(grow)
    bcum_row = jnp.dot(lf_row, upper, preferred_element_type=F32, precision=lax.Precision.HIGHEST)

    for h in range(nh):
        qh = q_ref[:, h * dqk:(h + 1) * dqk]
        kh = k_ref[:, h * dqk:(h + 1) * dqk]
        vh = v_ref[:, h * dv:(h + 1) * dv]
        bc = bcum_col[:, nh + h:nh + h + 1]
        ic = gcol[:, h:h + 1]
        brw = bcum_row[nh + h:nh + h + 1, :]
        ir = grow[h:h + 1, :]
        m_st = m_ref[h:h + 1, 0:1]
        c_st = c_ref[h]
        n_st = n_ref[h:h + 1, :]

        dmat = jnp.where(causal, bc - brw + ir, -jnp.inf)
        inter = bc + m_st
        m_row = jnp.maximum(jnp.max(dmat, axis=1, keepdims=True), inter)
        w = jnp.exp(dmat - m_row)
        w_inter = jnp.exp(inter - m_row)
        s = lax.dot_general(qh, kh, (((1,), (1,)), ((), ())), preferred_element_type=F32) * (w * kscale)
        num = (jnp.dot(s.astype(BF16), vh, preferred_element_type=F32)
               + w_inter * jnp.dot(qh, c_st.astype(BF16), preferred_element_type=F32))
        qn = jnp.sum(qh.astype(F32) * n_st, axis=1, keepdims=True)
        den = jnp.sum(s, axis=1, keepdims=True) + w_inter * qn
        h_t = num / jnp.maximum(jnp.abs(den), jnp.exp(-m_row))

        b_last = bc[t - 1:t, :]
        dec_row = b_last - brw + ir
        m_new = jnp.maximum(b_last + m_st, jnp.max(dec_row, axis=1, keepdims=True))
        dec_col = b_last - bc + ic
        wk = jnp.exp(dec_col - m_new) * kscale
        carry_scale = jnp.exp(b_last + m_st - m_new)
        kw = kh.astype(F32) * wk
        c_ref[h] = carry_scale * c_st + lax.dot_general(
            kw.astype(BF16), vh, (((0,), (0,)), ((), ())), preferred_element_type=F32)
        n_ref[h:h + 1, :] = carry_scale * n_st + jnp.sum(kw, axis=0, keepdims=True)
        m_ref[h:h + 1, :] = jnp.broadcast_to(m_new, (1, m_ref.shape[1]))

        ms = jnp.mean(h_t * h_t, axis=1, keepdims=True)
        hn = h_t * lax.rsqrt(ms + EPS) * hg_ref[:, h * dv:(h + 1) * dv]
        og = _sigmoid(o_ref[:, h * dv:(h + 1) * dv])
        out_ref[:, h * dv:(h + 1) * dv] = (og * hn).astype(out_ref.dtype)


def mlstm_scan(qkv, o, g_col, g_row, bias, head_gain):
    seq = qkv.shape[0]
    d = o.shape[1]
    nqk = (qkv.shape[1] - d) // 2
    nh = ML_HEADS
    assert d == 2 * nqk
    bias_col = jnp.zeros((1, LANES), F32).at[0, :2 * nh].set(bias)
    bias_row = bias.reshape(2 * nh, 1)
    return pl.pallas_call(
        _mlstm_kernel,
        out_shape=jax.ShapeDtypeStruct((seq, d), BF16),
        grid=(seq // ML_T,),
        in_specs=[
            pl.BlockSpec((ML_T, nqk), lambda c: (c, 0)),
            pl.BlockSpec((ML_T, nqk), lambda c: (c, 1)),
            pl.BlockSpec((ML_T, d), lambda c: (c, 1)),
            pl.BlockSpec((ML_T, d), lambda c: (c, 0)),
            pl.BlockSpec((ML_T, LANES), lambda c: (c, 0)),
            pl.BlockSpec((2 * nh, ML_T), lambda c: (0, c)),
            _const_spec((1, LANES)),
            _const_spec((2 * nh, 1)),
            _const_spec((1, d)),
        ],
        out_specs=pl.BlockSpec((ML_T, d), lambda c: (c, 0)),
        scratch_shapes=[
            pltpu.VMEM((nh, nqk // nh, d // nh), F32),
            pltpu.VMEM((nh, nqk // nh), F32),
            pltpu.VMEM((nh, LANES), F32),
        ],
        compiler_params=_cparams("arbitrary"),
        name="mlstm_scan",
    )(qkv, qkv, qkv, o, g_col, g_row, bias_col, bias_row, head_gain.reshape(1, d))


def _sb_layer(x, g, w_in, q_gain, k_gain, w_out):
    qkv = sb_inproj(x, g, w_in.astype(BF16), q_gain, k_gain)
    o = sb_attention(qkv)
    return resid_matmul(x, o, w_out.astype(BF16), name="sb_out")


def _block_diag(t):
    nb, gb, r, c = t.shape
    eye = jnp.eye(gb, dtype=t.dtype)
    return (t[:, :, :, None, :] * eye[None, :, None, :, None]).reshape(nb, gb * r, gb * c)


def _s5_layer(x, g, w_in, lam_re, lam_im, log_dt, b_re, b_im, c_re, c_im, d_skip, w_glu, w_out):
    ngrp, nst = lam_re.shape
    gs = b_re.shape[-1]
    gb = ngrp // S5_NBLK
    d = x.shape[1]
    u = norm_matmul(x, g, w_in.astype(BF16), F32, tm=512, tn=d, name="s5_in")
    a_re, a_im, bb_re, bb_im = s5_params(lam_re, lam_im, log_dt, b_re, b_im)
    to_b = lambda t: _block_diag(t.reshape(S5_NBLK, gb, nst, gs).transpose(0, 1, 3, 2)).astype(BF16)
    to_c = lambda t: _block_diag(t.reshape(S5_NBLK, gb, gs, nst).transpose(0, 1, 3, 2)).astype(BF16)
    y = s5_scan(u, to_b(bb_re), to_b(bb_im), to_c(c_re), to_c(c_im), a_re, a_im, d_skip)
    return glu_out(x, y, w_glu.astype(BF16), w_out.astype(BF16))


def _mlstm_layer(x, g, w_in, gate_bias, head_gain, w_out):
    d = x.shape[1]
    nh = ML_HEADS
    nqk = d // 2
    n_qkv, n_o = 2 * nqk + d, d
    w_bf = w_in.astype(BF16)
    w_g = jnp.pad(w_bf[:, n_qkv + n_o:], ((0, 0), (0, LANES - 2 * nh)))
    qkv, o, g_col = ml_inproj(x, g, w_bf, w_g, n_qkv, n_o)
    g_row = g_col[:, :2 * nh].T
    hs = mlstm_scan(qkv, o, g_col, g_row, gate_bias, head_gain)
    return resid_matmul(x, hs, w_out.astype(BF16), name="ml_out")


def kernel(x, p, norm_mix, norm_ffn, sb_w_in, sb_q_gain, sb_k_gain, sb_w_out, s5_w_in, s5_lam_re, s5_lam_im, s5_log_dt, s5_b_re, s5_b_im, s5_c_re, s5_c_im, s5_d, s5_w_glu, s5_w_out, ml_w_in, ml_gate_bias, ml_head_gain, ml_w_out, ffn_w_gate, ffn_w_up, ffn_w_down, ple_norm, ple_w_gate, ple_w_proj):
    bsz, seq, d = x.shape
    depth = norm_mix.shape[0]
    outs = []
    for b in range(bsz):
        h = x[b]
        for i in range(depth):
            kind, j = i % 3, i // 3
            if kind == 0:
                h = _sb_layer(h, norm_mix[i], sb_w_in[j], sb_q_gain[j], sb_k_gain[j], sb_w_out[j])
            elif kind == 1:
                h = _s5_layer(h, norm_mix[i], s5_w_in[j], s5_lam_re[j], s5_lam_im[j], s5_log_dt[j],
                              s5_b_re[j], s5_b_im[j], s5_c_re[j], s5_c_im[j], s5_d[j],
                              s5_w_glu[j], s5_w_out[j])
            else:
                h = _mlstm_layer(h, norm_mix[i], ml_w_in[j], ml_gate_bias[j], ml_head_gain[j],
                                 ml_w_out[j])
            h = ffn(h, norm_ffn[i], ffn_w_gate[i].astype(BF16), ffn_w_up[i].astype(BF16),
                    ffn_w_down[i].astype(BF16))
            h = ple(h, ple_norm[i], p[i, b], ple_w_gate[i].astype(BF16), ple_w_proj[i].astype(BF16))
        outs.append(h)
    return jnp.stack(outs)
```

```python
import functools
import math

import jax
import jax.numpy as jnp
from jax import lax
from jax.experimental import pallas as pl
from jax.experimental.pallas import tpu as pltpu

F32 = jnp.float32
BF16 = jnp.bfloat16

EPS = 1e-6
LANES = 128
SUBLANES = 8
VMEM_LIMIT = 56 * 1024 * 1024
COL_CHUNK = 512

SB_HEAD_DIM = 128
ML_HEADS = 8
EXP_ZERO_BELOW = -110.0


def _cparams(*sem):
    return pltpu.CompilerParams(dimension_semantics=sem, vmem_limit_bytes=VMEM_LIMIT)


def _const_spec(shape, layer=None):
    nd = len(shape)
    if layer is None:
        return pl.BlockSpec(shape, lambda *_: (0,) * nd, pipeline_mode=pl.Buffered(1))
    return pl.BlockSpec((None,) + tuple(shape), lambda *_: (layer,) + (0,) * nd,
                        pipeline_mode=pl.Buffered(1))


def _rms_to(dst_ref, x_ref, g_ref, copy_ref=None, chunk=256):
    rows = x_ref.shape[0]
    chunk = min(chunk, rows)

    def body(c, _):
        r = pl.multiple_of(c * chunk, chunk)
        x = x_ref[pl.ds(r, chunk), :]
        ms = jnp.mean(x * x, axis=-1, keepdims=True)
        dst_ref[pl.ds(r, chunk), :] = (x * lax.rsqrt(ms + EPS) * g_ref[...]).astype(dst_ref.dtype)
        if copy_ref is not None:
            copy_ref[pl.ds(r, chunk), :] = x
        return 0

    lax.fori_loop(0, rows // chunk, body, 0)


def _col_chunks(n):
    c = min(COL_CHUNK, n)
    assert n % c == 0
    return [slice(i * c, (i + 1) * c) for i in range(n // c)]


def _log_sigmoid(z):
    return jnp.minimum(z, 0.0) - jnp.log1p(jnp.exp(-jnp.abs(z)))


def _log_one_minus_sigmoid(z):
    return -jnp.maximum(z, 0.0) - jnp.log(1.0 + jnp.exp(-jnp.abs(z)))


def _sigmoid(z):
    return 1.0 / (1.0 + jnp.exp(-z))


def _norm_matmul_kernel(x_ref, g_ref, w_ref, o_ref, xn_ref):
    @pl.when(pl.program_id(1) == 0)
    def _():
        _rms_to(xn_ref, x_ref, g_ref)

    xn = xn_ref[...]
    for sl in _col_chunks(o_ref.shape[1]):
        o_ref[:, sl] = jnp.dot(xn, w_ref[:, sl], preferred_element_type=F32).astype(o_ref.dtype)


def norm_matmul(x, g, w, layer, out_dtype, *, tm, tn, name):
    m, k = x.shape
    n = w.shape[2]
    assert m % tm == 0 and n % tn == 0
    return pl.pallas_call(
        _norm_matmul_kernel,
        out_shape=jax.ShapeDtypeStruct((m, n), out_dtype),
        grid=(m // tm, n // tn),
        in_specs=[
            pl.BlockSpec((tm, k), lambda i, j: (i, 0)),
            pl.BlockSpec((1, k), lambda i, j: (0, 0)),
            pl.BlockSpec((None, k, tn), lambda i, j: (layer, 0, j)),
        ],
        out_specs=pl.BlockSpec((tm, tn), lambda i, j: (i, j)),
        scratch_shapes=[pltpu.VMEM((tm, k), BF16)],
        compiler_params=_cparams("parallel", "arbitrary"),
        name=name,
    )(x, g.reshape(1, k), w)


def _sb_inproj_kernel(x_ref, g_ref, w_ref, hg_ref, o_ref, xn_ref, *, n_qk_tiles, n_q_tiles):
    j = pl.program_id(1)

    @pl.when(j == 0)
    def _():
        _rms_to(xn_ref, x_ref, g_ref)

    y = jnp.dot(xn_ref[...], w_ref[...], preferred_element_type=F32)
    tn = y.shape[1]

    @pl.when(j < n_qk_tiles)
    def _():
        scale = jnp.where(j < n_q_tiles, SB_HEAD_DIM ** -0.5, 1.0).astype(F32)
        for h in range(tn // SB_HEAD_DIM):
            blk = y[:, h * SB_HEAD_DIM:(h + 1) * SB_HEAD_DIM]
            ms = jnp.mean(blk * blk, axis=-1, keepdims=True)
            nb = blk * lax.rsqrt(ms + EPS) * hg_ref[...]
            o_ref[:, h * SB_HEAD_DIM:(h + 1) * SB_HEAD_DIM] = (nb * scale).astype(o_ref.dtype)

    @pl.when(j >= n_qk_tiles)
    def _():
        o_ref[...] = y.astype(o_ref.dtype)


def sb_inproj(x, g, w, layer, q_gain, k_gain, *, tm=1024, tn=512):
    m, k = x.shape
    n = w.shape[2]
    d = n // 3
    n_q_tiles = d // tn
    hg = jnp.stack([q_gain, k_gain]).reshape(2, 1, SB_HEAD_DIM)
    kern = functools.partial(_sb_inproj_kernel, n_qk_tiles=2 * n_q_tiles, n_q_tiles=n_q_tiles)
    return pl.pallas_call(
        kern,
        out_shape=jax.ShapeDtypeStruct((m, n), BF16),
        grid=(m // tm, n // tn),
        in_specs=[
            pl.BlockSpec((tm, k), lambda i, j: (i, 0)),
            pl.BlockSpec((1, k), lambda i, j: (0, 0)),
            pl.BlockSpec((None, k, tn), lambda i, j: (layer, 0, j)),
            pl.BlockSpec((None, 1, SB_HEAD_DIM), lambda i, j: (jnp.minimum(j // n_q_tiles, 1), 0, 0)),
        ],
        out_specs=pl.BlockSpec((tm, tn), lambda i, j: (i, j)),
        scratch_shapes=[pltpu.VMEM((tm, k), BF16)],
        compiler_params=_cparams("parallel", "arbitrary"),
        name="sb_inproj",
    )(x, g.reshape(1, k), w, hg)


def _sb_attn_kernel(q_ref, k_ref, v_ref, suf_ref, o_ref, *, blk, nsub):
    qi = pl.program_id(1)
    row = lax.broadcasted_iota(jnp.int32, (blk, blk), 0)
    col = lax.broadcasted_iota(jnp.int32, (blk, blk), 1)
    strict = col < row
    suffix = suf_ref[...]

    subs = range(nsub)

    def suffix_sums(lgs):
        his = [lg.astype(BF16) for lg in lgs]
        los = [(lg - hi.astype(F32)).astype(BF16) for lg, hi in zip(lgs, his)]
        return [jnp.dot(hi, suffix, preferred_element_type=F32)
                + jnp.dot(lo, suffix, preferred_element_type=F32) for hi, lo in zip(his, los)]

    def scores(q, kb):
        r = pl.multiple_of(kb * blk, blk)
        k = k_ref[pl.ds(r, blk), :]
        z = lax.dot_general(q, k, (((1,), (1,)), ((), ())), preferred_element_type=F32)
        return z, v_ref[pl.ds(r, blk), :]

    def earlier_blocks(kbs, carries):
        zv = [scores(qs[j], jnp.maximum(kbs[j], 0)) for j in subs]
        cs = suffix_sums([_log_one_minus_sigmoid(z) for z, _ in zv])
        a = [jnp.where(kbs[j] >= 0, jnp.exp(zv[j][0] + cs[j] + carries[j]), 0.0) for j in subs]
        new_c = [carries[j] + jnp.where(kbs[j] >= 0, cs[j][:, 0:1], 0.0) for j in subs]
        pv = [jnp.dot(a[j].astype(BF16), zv[j][1], preferred_element_type=F32) for j in subs]
        return new_c, pv

    qs = [q_ref[j * blk:(j + 1) * blk, :] for j in subs]
    gbs = [qi * nsub + j for j in subs]
    zv = [scores(qs[j], gbs[j]) for j in subs]
    cs = suffix_sums([jnp.where(strict, _log_one_minus_sigmoid(z), 0.0) for z, _ in zv])
    a = [jnp.where(strict, jnp.exp(zv[j][0] + cs[j]), 0.0) for j in subs]
    acc_own = [jnp.dot(a[j].astype(BF16), zv[j][1], preferred_element_type=F32) for j in subs]
    carries, acc_prev = earlier_blocks([gb - 1 for gb in gbs], [c[:, 0:1] for c in cs])
    accs = [acc_own[j] + acc_prev[j] for j in subs]

    n_more = qi * nsub + nsub - 2

    def cond(state):
        i, carries, _ = state
        live = carries[0]
        for c in carries[1:]:
            live = jnp.maximum(live, c)
        return jnp.logical_and(i < n_more, jnp.max(live) > EXP_ZERO_BELOW)

    def body(state):
        i, carries, accs = state
        new_c, pv = earlier_blocks([gbs[j] - 2 - i for j in subs], carries)
        return i + 1, tuple(new_c), tuple(accs[j] + pv[j] for j in subs)

    _, _, accs = lax.while_loop(cond, body, (jnp.int32(0), tuple(carries), tuple(accs)))
    for j in subs:
        o_ref[j * blk:(j + 1) * blk, :] = accs[j].astype(o_ref.dtype)


def sb_attention(qkv, *, blk=256, nsub=4):
    seq, n3 = qkv.shape
    d = n3 // 3
    heads = d // SB_HEAD_DIM
    tq = blk * nsub
    assert seq % tq == 0
    idx = jnp.arange(blk)
    suffix = (idx[:, None] >= idx[None, :]).astype(BF16)
    kern = functools.partial(_sb_attn_kernel, blk=blk, nsub=nsub)
    return pl.pallas_call(
        kern,
        out_shape=jax.ShapeDtypeStruct((seq, d), BF16),
        grid=(heads, seq // tq),
        in_specs=[
            pl.BlockSpec((tq, SB_HEAD_DIM), lambda h, i: (i, h)),
            pl.BlockSpec((seq, SB_HEAD_DIM), lambda h, i: (0, heads + h)),
            pl.BlockSpec((seq, SB_HEAD_DIM), lambda h, i: (0, 2 * heads + h)),
            _const_spec((blk, blk)),
        ],
        out_specs=pl.BlockSpec((tq, SB_HEAD_DIM), lambda h, i: (i, h)),
        compiler_params=_cparams("parallel", "arbitrary"),
        name="sb_attention",
    )(qkv, qkv, qkv, suffix)


def _resid_matmul_kernel(x_ref, a_ref, w_ref, o_ref):
    a = a_ref[...]
    for sl in _col_chunks(o_ref.shape[1]):
        o_ref[:, sl] = x_ref[:, sl] + jnp.dot(a, w_ref[:, sl], preferred_element_type=F32)


def resid_matmul(x, a, w, layer, *, tm=512, name="resid_matmul"):
    m, n = x.shape
    k = a.shape[1]
    return pl.pallas_call(
        _resid_matmul_kernel,
        out_shape=jax.ShapeDtypeStruct((m, n), F32),
        grid=(m // tm,),
        in_specs=[
            pl.BlockSpec((tm, n), lambda i: (i, 0)),
            pl.BlockSpec((tm, k), lambda i: (i, 0)),
            _const_spec((k, n), layer),
        ],
        out_specs=pl.BlockSpec((tm, n), lambda i: (i, 0)),
        compiler_params=_cparams("parallel"),
        name=name,
    )(x, a, w)


def _ffn_kernel(x_ref, g_ref, wg_ref, wu_ref, wd_ref, o_ref, xn_ref):
    @pl.when(pl.program_id(1) == 0)
    def _():
        _rms_to(xn_ref, x_ref, g_ref, copy_ref=o_ref)

    xn = xn_ref[...]
    gate = jnp.dot(xn, wg_ref[...], preferred_element_type=F32)
    up = jnp.dot(xn, wu_ref[...], preferred_element_type=F32)
    hid = (gate * _sigmoid(gate) * up).astype(BF16)
    o_ref[...] += jnp.dot(hid, wd_ref[...], preferred_element_type=F32)


def ffn(x, g, w_gate, w_up, w_down, layer, *, tm=1024, tf=512):
    m, d = x.shape
    dff = w_gate.shape[2]
    assert m % tm == 0 and dff % tf == 0
    return pl.pallas_call(
        _ffn_kernel,
        out_shape=jax.ShapeDtypeStruct((m, d), F32),
        grid=(m // tm, dff // tf),
        in_specs=[
            pl.BlockSpec((tm, d), lambda i, f: (i, 0)),
            pl.BlockSpec((1, d), lambda i, f: (0, 0)),
            pl.BlockSpec((None, d, tf), lambda i, f: (layer, 0, f)),
            pl.BlockSpec((None, d, tf), lambda i, f: (layer, 0, f)),
            pl.BlockSpec((None, tf, d), lambda i, f: (layer, f, 0)),
        ],
        out_specs=pl.BlockSpec((tm, d), lambda i, f: (i, 0)),
        scratch_shapes=[pltpu.VMEM((tm, d), BF16)],
        compiler_params=_cparams("parallel", "arbitrary"),
        name="ffn",
    )(x, g.reshape(1, d), w_gate, w_up, w_down)


def _ple_kernel(x_ref, g_ref, p_ref, wg_ref, wp_ref, o_ref, xn_ref):
    _rms_to(xn_ref, x_ref, g_ref)
    xn = xn_ref[...]
    pb = p_ref[...].astype(BF16)
    for sl in _col_chunks(o_ref.shape[1]):
        gate = _sigmoid(jnp.dot(xn, wg_ref[:, sl], preferred_element_type=F32))
        proj = jnp.dot(pb, wp_ref[:, sl], preferred_element_type=F32)
        o_ref[:, sl] = x_ref[:, sl] + gate * proj


def ple(x, g, p, w_gate, w_proj, layer, batch, *, tm=512):
    m, d = x.shape
    pd = p.shape[3]
    return pl.pallas_call(
        _ple_kernel,
        out_shape=jax.ShapeDtypeStruct((m, d), F32),
        grid=(m // tm,),
        in_specs=[
            pl.BlockSpec((tm, d), lambda i: (i, 0)),
            _const_spec((1, d)),
            pl.BlockSpec((None, None, tm, pd), lambda i: (layer, batch, i, 0)),
            _const_spec((d, d), layer),
            _const_spec((pd, d), layer),
        ],
        out_specs=pl.BlockSpec((tm, d), lambda i: (i, 0)),
        scratch_shapes=[pltpu.VMEM((tm, d), BF16)],
        compiler_params=_cparams("parallel"),
        name="ple",
    )(x, g.reshape(1, d), p, w_gate, w_proj)


def _s5_disc(lr, li, dt):
    mag = jnp.exp(lr * dt)
    ar = mag * jnp.cos(li * dt)
    ai = mag * jnp.sin(li * dt)
    den = lr * lr + li * li
    cr = ((ar - 1.0) * lr + ai * li) / den
    ci = (ai * lr - (ar - 1.0) * li) / den
    return ar, ai, cr, ci


def _s5_params_kernel(lr_ref, li_ref, ldt_ref, lrx_ref, lix_ref, ldtx_ref, bre_ref, bim_ref,
                      ar_ref, ai_ref, bbr_ref, bbi_ref):
    ar, ai, _, _ = _s5_disc(lr_ref[...], li_ref[...], jnp.exp(ldt_ref[...]))
    ar_ref[...] = ar
    ai_ref[...] = ai
    _, _, cr, ci = _s5_disc(lrx_ref[...], lix_ref[...], jnp.exp(ldtx_ref[...]))
    br = bre_ref[...]
    bi = bim_ref[...]
    bbr_ref[...] = cr * br - ci * bi
    bbi_ref[...] = cr * bi + ci * br


def s5_params(lam_re, lam_im, log_dt, b_re, b_im):
    g, p = lam_re.shape
    gs = b_re.shape[-1]
    ldt = jnp.broadcast_to(log_dt[:, None], (g, p))
    rep = lambda t: jnp.repeat(t, gs, axis=1)
    shp = jax.ShapeDtypeStruct
    return pl.pallas_call(
        _s5_params_kernel,
        out_shape=(shp((g, p), F32), shp((g, p), F32), shp((g, p * gs), F32), shp((g, p * gs), F32)),
        name="s5_params",
    )(lam_re, lam_im, ldt, rep(lam_re), rep(lam_im), rep(ldt),
      b_re.reshape(g, p * gs), b_im.reshape(g, p * gs))


S5_NBLK = 8
S5_T = 128
S5_PITCH = S5_T + 4


def _gelu_tanh(y):
    return 0.5 * y * (1.0 + jnp.tanh(math.sqrt(2.0 / math.pi) * (y + 0.044715 * (y * y * y))))


def _s5_scan_kernel(u_ref, bre_ref, bim_ref, cre_ref, cim_ref, ar_ref, ai_ref, d_ref, o_ref,
                    br_ref, bi_ref, xr_ref, xi_ref, sr_ref, si_ref):
    nslab = xr_ref.shape[0]
    ucols = u_ref.shape[1] // S5_NBLK

    @pl.when(pl.program_id(0) == 0)
    def _():
        sr_ref[...] = jnp.zeros_like(sr_ref)
        si_ref[...] = jnp.zeros_like(si_ref)

    for kb in range(S5_NBLK):
        ukb = u_ref[:, kb * ucols:(kb + 1) * ucols].astype(BF16)
        bur = jnp.dot(ukb, bre_ref[kb], preferred_element_type=F32)
        bui = jnp.dot(ukb, bim_ref[kb], preferred_element_type=F32)
        for cs in range(nslab):
            br_ref[cs, kb * S5_PITCH:kb * S5_PITCH + S5_T, :] = bur[:, cs * LANES:(cs + 1) * LANES]
            bi_ref[cs, kb * S5_PITCH:kb * S5_PITCH + S5_T, :] = bui[:, cs * LANES:(cs + 1) * LANES]

    def step(t, carry):
        new = []
        for cs in range(nslab):
            sr, si = carry[cs]
            ar = ar_ref[:, cs * LANES:(cs + 1) * LANES]
            ai = ai_ref[:, cs * LANES:(cs + 1) * LANES]
            rows = pl.ds(t, S5_NBLK, stride=S5_PITCH)
            nr = ar * sr - ai * si + br_ref[cs, rows, :]
            ni = ar * si + ai * sr + bi_ref[cs, rows, :]
            xr_ref[cs, rows, :] = nr
            xi_ref[cs, rows, :] = ni
            new.append((nr, ni))
        return tuple(new)

    init = tuple((sr_ref[:, cs * LANES:(cs + 1) * LANES], si_ref[:, cs * LANES:(cs + 1) * LANES])
                 for cs in range(nslab))
    final = lax.fori_loop(0, S5_T, step, init, unroll=2)
    for cs in range(nslab):
        sr_ref[:, cs * LANES:(cs + 1) * LANES] = final[cs][0]
        si_ref[:, cs * LANES:(cs + 1) * LANES] = final[cs][1]

    for kb in range(S5_NBLK):
        rows = slice(kb * S5_PITCH, kb * S5_PITCH + S5_T)
        xr = jnp.concatenate([xr_ref[cs, rows, :] for cs in range(nslab)], axis=1).astype(BF16)
        xi = jnp.concatenate([xi_ref[cs, rows, :] for cs in range(nslab)], axis=1).astype(BF16)
        y = (jnp.dot(xr, cre_ref[kb], preferred_element_type=F32)
             - jnp.dot(xi, cim_ref[kb], preferred_element_type=F32))
        sl = slice(kb * ucols, (kb + 1) * ucols)
        y = y + d_ref[:, sl] * u_ref[:, sl]
        o_ref[:, sl] = _gelu_tanh(y)


def s5_scan(u, bblk_re, bblk_im, cblk_re, cblk_im, a_re, a_im, d_skip):
    seq, d = u.shape
    nstate = a_re.size
    blk_states = nstate // S5_NBLK
    nslab = blk_states // LANES
    ucols = d // S5_NBLK
    return pl.pallas_call(
        _s5_scan_kernel,
        out_shape=jax.ShapeDtypeStruct((seq, d), F32),
        grid=(seq // S5_T,),
        in_specs=[
            pl.BlockSpec((S5_T, d), lambda c: (c, 0)),
            _const_spec((S5_NBLK, ucols, blk_states)),
            _const_spec((S5_NBLK, ucols, blk_states)),
            _const_spec((S5_NBLK, blk_states, ucols)),
            _const_spec((S5_NBLK, blk_states, ucols)),
            _const_spec((S5_NBLK, blk_states)),
            _const_spec((S5_NBLK, blk_states)),
            _const_spec((1, d)),
        ],
        out_specs=pl.BlockSpec((S5_T, d), lambda c: (c, 0)),
        scratch_shapes=[
            pltpu.VMEM((nslab, S5_NBLK * S5_PITCH, LANES), F32),
            pltpu.VMEM((nslab, S5_NBLK * S5_PITCH, LANES), F32),
            pltpu.VMEM((nslab, S5_NBLK * S5_PITCH, LANES), F32),
            pltpu.VMEM((nslab, S5_NBLK * S5_PITCH, LANES), F32),
            pltpu.VMEM((S5_NBLK, blk_states), F32),
            pltpu.VMEM((S5_NBLK, blk_states), F32),
        ],
        compiler_params=_cparams("arbitrary"),
        name="s5_scan",
    )(u, bblk_re, bblk_im, cblk_re, cblk_im,
      a_re.reshape(S5_NBLK, blk_states), a_im.reshape(S5_NBLK, blk_states), d_skip.reshape(1, d))


def _glu_out_kernel(x_ref, g_ref, wglu_ref, wout_ref, o_ref, h_ref):
    gb = g_ref[...].astype(BF16)
    chunks = _col_chunks(o_ref.shape[1])
    for sl in chunks:
        gate = _sigmoid(jnp.dot(gb, wglu_ref[:, sl], preferred_element_type=F32))
        h_ref[:, sl] = (g_ref[:, sl] * gate).astype(BF16)
    h = h_ref[...]
    for sl in chunks:
        o_ref[:, sl] = x_ref[:, sl] + jnp.dot(h, wout_ref[:, sl], preferred_element_type=F32)


def glu_out(x, g, w_glu, w_out, layer, *, tm=512):
    m, d = x.shape
    return pl.pallas_call(
        _glu_out_kernel,
        out_shape=jax.ShapeDtypeStruct((m, d), F32),
        grid=(m // tm,),
        in_specs=[
            pl.BlockSpec((tm, d), lambda i: (i, 0)),
            pl.BlockSpec((tm, d), lambda i: (i, 0)),
            _const_spec((d, d), layer),
            _const_spec((d, d), layer),
        ],
        out_specs=pl.BlockSpec((tm, d), lambda i: (i, 0)),
        scratch_shapes=[pltpu.VMEM((tm, d), BF16)],
        compiler_params=_cparams("parallel"),
        name="s5_glu_out",
    )(x, g, w_glu, w_out)


def _ml_inproj_kernel(x_ref, g_ref, w_ref, wg_ref, qkv_ref, o_ref, gates_ref, xn_ref, *, n_qkv_tiles):
    j = pl.program_id(1)

    @pl.when(j == 0)
    def _():
        _rms_to(xn_ref, x_ref, g_ref)
        gates_ref[...] = jnp.dot(xn_ref[...], wg_ref[...], preferred_element_type=F32)

    y = jnp.dot(xn_ref[...], w_ref[...], preferred_element_type=F32)

    @pl.when(j < n_qkv_tiles)
    def _():
        qkv_ref[...] = y.astype(qkv_ref.dtype)

    @pl.when(j >= n_qkv_tiles)
    def _():
        o_ref[...] = y


def ml_inproj(x, g, w, layer, w_gates, n_qkv, n_o, *, tm=1024, tn=512):
    m, k = x.shape
    tq, to = n_qkv // tn, n_o // tn
    kern = functools.partial(_ml_inproj_kernel, n_qkv_tiles=tq)
    shp = jax.ShapeDtypeStruct
    return pl.pallas_call(
        kern,
        out_shape=(shp((m, n_qkv), BF16), shp((m, n_o), F32), shp((m, LANES), F32)),
        grid=(m // tm, tq + to),
        in_specs=[
            pl.BlockSpec((tm, k), lambda i, j: (i, 0)),
            pl.BlockSpec((1, k), lambda i, j: (0, 0)),
            pl.BlockSpec((None, k, tn), lambda i, j: (layer, 0, j)),
            _const_spec((k, LANES)),
        ],
        out_specs=(
            pl.BlockSpec((tm, tn), lambda i, j: (i, jnp.minimum(j, tq - 1))),
            pl.BlockSpec((tm, tn), lambda i, j: (i, jnp.maximum(j - tq, 0))),
            pl.BlockSpec((tm, LANES), lambda i, j: (i, 0)),
        ),
        scratch_shapes=[pltpu.VMEM((tm, k), BF16)],
        compiler_params=_cparams("parallel", "arbitrary"),
        name="ml_inproj",
    )(x, g.reshape(1, k), w, w_gates)


ML_T = 128


def _mlstm_kernel(q_ref, k_ref, v_ref, o_ref, gc_ref, gr_ref, bc_ref, br_ref, hg_ref, out_ref,
                  c_ref, m_ref):
    t = ML_T
    nh = ML_HEADS
    dqk = q_ref.shape[1] // nh
    dv = v_ref.shape[1] // nh
    kscale = dqk ** -0.5

    @pl.when(pl.program_id(0) == 0)
    def _():
        c_ref[...] = jnp.zeros_like(c_ref)
        m_ref[...] = jnp.zeros_like(m_ref)

    row = lax.broadcasted_iota(jnp.int32, (t, t), 0)
    col = lax.broadcasted_iota(jnp.int32, (t, t), 1)
    causal = col <= row
    lower = causal.astype(F32)
    upper = (row <= col).astype(F32)
    eye = (lax.broadcasted_iota(jnp.int32, (dqk, dqk), 0)
           == lax.broadcasted_iota(jnp.int32, (dqk, dqk), 1)).astype(BF16)
    ones_col = (lax.broadcasted_iota(jnp.int32, (t, LANES), 1) == 0).astype(BF16)

    gcol = gc_ref[...] + bc_ref[...]
    lf_col = _log_sigmoid(gcol)
    bcum_col = jnp.dot(lower, lf_col, preferred_element_type=F32, precision=lax.Precision.HIGHEST)
    grow = gr_ref[...] + br_ref[...]
    lf_row = _log_sigmoid(grow)
    bcum_row = jnp.dot(lf_row, upper, preferred_element_type=F32, precision=lax.Precision.HIGHEST)

    m_all = m_ref[...]
    heads = range(nh)
    nt = (((1,), (1,)), ((), ()))
    qs = [q_ref[:, h * dqk:(h + 1) * dqk] for h in heads]
    ks = [k_ref[:, h * dqk:(h + 1) * dqk] for h in heads]
    vs = [v_ref[:, h * dv:(h + 1) * dv] for h in heads]
    cs = [c_ref[h] for h in heads]
    bcs = [bcum_col[:, nh + h:nh + h + 1] for h in heads]
    brs = [bcum_row[nh + h:nh + h + 1, :] for h in heads]
    irs = [grow[h:h + 1, :] for h in heads]
    ms_ = [m_all[h:h + 1, 0:1] for h in heads]

    qk = [lax.dot_general(qs[h], ks[h], nt, preferred_element_type=F32) for h in heads]
    qc = [jnp.dot(qs[h], cs[h].astype(BF16), preferred_element_type=F32) for h in heads]
    kt = [lax.dot_general(eye, ks[h], nt, preferred_element_type=F32) for h in heads]

    dmat = [jnp.where(causal, bcs[h] - brs[h] + irs[h], -jnp.inf) for h in heads]
    inter = [bcs[h] + ms_[h] for h in heads]
    m_row = [jnp.maximum(jnp.max(dmat[h], axis=1, keepdims=True), inter[h]) for h in heads]
    s = [qk[h] * (jnp.exp(dmat[h] - m_row[h]) * kscale) for h in heads]
    w_inter = [jnp.exp(inter[h] - m_row[h]) for h in heads]
    sv = [jnp.dot(s[h].astype(BF16), vs[h], preferred_element_type=F32) for h in heads]

    b_last = [brs[h][:, t - 1:t] for h in heads]
    dec = [b_last[h] - brs[h] + irs[h] for h in heads]
    m_new = [jnp.maximum(b_last[h] + ms_[h], jnp.max(dec[h], axis=1, keepdims=True)) for h in heads]
    wk = [jnp.exp(dec[h] - m_new[h]) * kscale for h in heads]
    carry_scale = [jnp.exp(b_last[h] + ms_[h] - m_new[h]) for h in heads]
    for h in heads:
        v_ext = jnp.concatenate([vs[h], ones_col], axis=1)
        c_ref[h] = carry_scale[h] * cs[h] + jnp.dot((kt[h] * wk[h]).astype(BF16), v_ext,
                                                    preferred_element_type=F32)
    m_ref[...] = jnp.concatenate([jnp.broadcast_to(m_new[h], (1, m_ref.shape[1])) for h in heads],
                                 axis=0)

    num = [sv[h] + w_inter[h] * qc[h][:, :dv] for h in heads]
    den = [jnp.sum(s[h], axis=1, keepdims=True) + w_inter[h] * qc[h][:, dv:dv + 1] for h in heads]
    h_t = [num[h] / jnp.maximum(jnp.abs(den[h]), jnp.exp(-m_row[h])) for h in heads]
    msq = [jnp.mean(h_t[h] * h_t[h], axis=1, keepdims=True) for h in heads]
    for h in heads:
        hn = h_t[h] * lax.rsqrt(msq[h] + EPS) * hg_ref[:, h * dv:(h + 1) * dv]
        og = _sigmoid(o_ref[:, h * dv:(h + 1) * dv])
        out_ref[:, h * dv:(h + 1) * dv] = (og * hn).astype(out_ref.dtype)


def mlstm_scan(qkv, o, g_col, g_row, bias, head_gain):
    seq = qkv.shape[0]
    d = o.shape[1]
    nqk = (qkv.shape[1] - d) // 2
    nh = ML_HEADS
    assert d == 2 * nqk
    bias_col = jnp.zeros((1, LANES), F32).at[0, :2 * nh].set(bias)
    bias_row = bias.reshape(2 * nh, 1)
    return pl.pallas_call(
        _mlstm_kernel,
        out_shape=jax.ShapeDtypeStruct((seq, d), BF16),
        grid=(seq // ML_T,),
        in_specs=[
            pl.BlockSpec((ML_T, nqk), lambda c: (c, 0)),
            pl.BlockSpec((ML_T, nqk), lambda c: (c, 1)),
            pl.BlockSpec((ML_T, d), lambda c: (c, 1)),
            pl.BlockSpec((ML_T, d), lambda c: (c, 0)),
            pl.BlockSpec((ML_T, LANES), lambda c: (c, 0)),
            pl.BlockSpec((2 * nh, ML_T), lambda c: (0, c)),
            _const_spec((1, LANES)),
            _const_spec((2 * nh, 1)),
            _const_spec((1, d)),
        ],
        out_specs=pl.BlockSpec((ML_T, d), lambda c: (c, 0)),
        scratch_shapes=[
            pltpu.VMEM((nh, nqk // nh, d // nh + LANES), F32),
            pltpu.VMEM((nh, LANES), F32),
        ],
        compiler_params=_cparams("arbitrary"),
        name="mlstm_scan",
    )(qkv, qkv, qkv, o, g_col, g_row, bias_col, bias_row, head_gain.reshape(1, d))


def _sb_layer(x, g, w_in, j, q_gain, k_gain, w_out):
    qkv = sb_inproj(x, g, w_in, j, q_gain, k_gain)
    o = sb_attention(qkv)
    return resid_matmul(x, o, w_out, j, name="sb_out")


def _block_diag(t):
    nb, gb, r, c = t.shape
    eye = jnp.eye(gb, dtype=t.dtype)
    return (t[:, :, :, None, :] * eye[None, :, None, :, None]).reshape(nb, gb * r, gb * c)


def _s5_layer(x, g, w_in, j, lam_re, lam_im, log_dt, b_re, b_im, c_re, c_im, d_skip, w_glu, w_out):
    ngrp, nst = lam_re.shape
    gs = b_re.shape[-1]
    gb = ngrp // S5_NBLK
    d = x.shape[1]
    u = norm_matmul(x, g, w_in, j, F32, tm=512, tn=d, name="s5_in")
    a_re, a_im, bb_re, bb_im = s5_params(lam_re, lam_im, log_dt, b_re, b_im)
    to_b = lambda t: _block_diag(t.astype(BF16).reshape(S5_NBLK, gb, nst, gs).transpose(0, 1, 3, 2))
    to_c = lambda t: _block_diag(t.astype(BF16).reshape(S5_NBLK, gb, gs, nst).transpose(0, 1, 3, 2))
    y = s5_scan(u, to_b(bb_re), to_b(bb_im), to_c(c_re), to_c(c_im), a_re, a_im, d_skip)
    return glu_out(x, y, w_glu, w_out, j)


def _mlstm_layer(x, g, w_in, j, gate_bias, head_gain, w_out):
    d = x.shape[1]
    nh = ML_HEADS
    nqk = d // 2
    n_qkv, n_o = 2 * nqk + d, d
    w_g = jnp.pad(w_in[j, :, n_qkv + n_o:], ((0, 0), (0, LANES - 2 * nh)))
    qkv, o, g_col = ml_inproj(x, g, w_in, j, w_g, n_qkv, n_o)
    g_row = g_col[:, :2 * nh].T
    hs = mlstm_scan(qkv, o, g_col, g_row, gate_bias, head_gain)
    return resid_matmul(x, hs, w_out, j, name="ml_out")


def kernel(x, p, norm_mix, norm_ffn, sb_w_in, sb_q_gain, sb_k_gain, sb_w_out, s5_w_in, s5_lam_re, s5_lam_im, s5_log_dt, s5_b_re, s5_b_im, s5_c_re, s5_c_im, s5_d, s5_w_glu, s5_w_out, ml_w_in, ml_gate_bias, ml_head_gain, ml_w_out, ffn_w_gate, ffn_w_up, ffn_w_down, ple_norm, ple_w_gate, ple_w_proj):
    bsz, seq, d = x.shape
    depth = norm_mix.shape[0]
    bf = lambda t: t.astype(BF16)
    sb_w_in, sb_w_out, s5_w_in, s5_w_glu, s5_w_out = map(bf, (sb_w_in, sb_w_out, s5_w_in, s5_w_glu, s5_w_out))
    ml_w_in, ml_w_out, ffn_w_gate, ffn_w_up, ffn_w_down = map(bf, (ml_w_in, ml_w_out, ffn_w_gate, ffn_w_up, ffn_w_down))
    ple_w_gate, ple_w_proj = bf(ple_w_gate), bf(ple_w_proj)
    outs = []
    for b in range(bsz):
        h = x[b]
        for i in range(depth):
            kind, j = i % 3, i // 3
            if kind == 0:
                h = _sb_layer(h, norm_mix[i], sb_w_in, j, sb_q_gain[j], sb_k_gain[j], sb_w_out)
            elif kind == 1:
                h = _s5_layer(h, norm_mix[i], s5_w_in, j, s5_lam_re[j], s5_lam_im[j], s5_log_dt[j],
                              s5_b_re[j], s5_b_im[j], s5_c_re[j], s5_c_im[j], s5_d[j],
                              s5_w_glu, s5_w_out)
            else:
                h = _mlstm_layer(h, norm_mix[i], ml_w_in, j, ml_gate_bias[j], ml_head_gain[j], ml_w_out)
            h = ffn(h, norm_ffn[i], ffn_w_gate, ffn_w_up, ffn_w_down, i)
            h = ple(h, ple_norm[i], p, ple_w_gate, ple_w_proj, i, b)
        outs.append(h)
    return jnp.stack(outs)
```

```python
import functools
import math

import jax
import jax.numpy as jnp
from jax import lax
from jax.experimental import pallas as pl
from jax.experimental.pallas import tpu as pltpu

F32 = jnp.float32
BF16 = jnp.bfloat16

EPS = 1e-6
LANES = 128
SUBLANES = 8
VMEM_LIMIT = 56 * 1024 * 1024
COL_CHUNK = 512

SB_HEAD_DIM = 128
ML_HEADS = 8
EXP2_ZERO_ABOVE = 160.0


def _cparams(*sem):
    return pltpu.CompilerParams(dimension_semantics=sem, vmem_limit_bytes=VMEM_LIMIT)


def _const_spec(shape, layer=None):
    nd = len(shape)
    if layer is None:
        return pl.BlockSpec(shape, lambda *_: (0,) * nd, pipeline_mode=pl.Buffered(1))
    return pl.BlockSpec((None,) + tuple(shape), lambda *_: (layer,) + (0,) * nd,
                        pipeline_mode=pl.Buffered(1))


BF16_SUBLANES = 16


def _side_cast_plan(stacks, layer, n_steps, step_of):
    in_specs, out_specs, out_shapes = [], [], []
    for w in stacks:
        _, r, c = w.shape
        n_slabs = n_steps
        while r % (n_slabs * BF16_SUBLANES):
            n_slabs //= 2
        rows, per = r // n_slabs, n_steps // n_slabs
        in_specs.append(pl.BlockSpec((None, rows, c), lambda *g, per=per: (layer, step_of(*g) // per, 0)))
        out_specs.append(pl.BlockSpec((rows, c), lambda *g, per=per: (step_of(*g) // per, 0)))
        out_shapes.append(jax.ShapeDtypeStruct((r, c), BF16))
    return in_specs, out_specs, out_shapes


def _side_cast(in_refs, out_refs):
    for wi, wo in zip(in_refs, out_refs):
        wo[...] = wi[...].astype(wo.dtype)


def _rms_to(dst_ref, x_ref, g_ref, copy_ref=None, chunk=256):
    rows = x_ref.shape[0]
    chunk = min(chunk, rows)

    def body(c, _):
        r = pl.multiple_of(c * chunk, chunk)
        x = x_ref[pl.ds(r, chunk), :]
        ms = jnp.mean(x * x, axis=-1, keepdims=True)
        dst_ref[pl.ds(r, chunk), :] = (x * lax.rsqrt(ms + EPS) * g_ref[...]).astype(dst_ref.dtype)
        if copy_ref is not None:
            copy_ref[pl.ds(r, chunk), :] = x
        return 0

    lax.fori_loop(0, rows // chunk, body, 0)


def _col_chunks(n):
    c = min(COL_CHUNK, n)
    assert n % c == 0
    return [slice(i * c, (i + 1) * c) for i in range(n // c)]


def _log_sigmoid(z):
    return jnp.minimum(z, 0.0) - jnp.log1p(jnp.exp(-jnp.abs(z)))


LOG2_E = 1.4426950408889634


def _softplus_log2(zz):
    return jnp.maximum(zz, 0.0) + jnp.log2(1.0 + jnp.exp2(-jnp.abs(zz)))


def _sigmoid(z):
    return 1.0 / (1.0 + jnp.exp(-z))


def _norm_matmul_kernel(x_ref, g_ref, w_ref, o_ref, xn_ref):
    @pl.when(pl.program_id(1) == 0)
    def _():
        _rms_to(xn_ref, x_ref, g_ref)

    xn = xn_ref[...]
    for sl in _col_chunks(o_ref.shape[1]):
        o_ref[:, sl] = jnp.dot(xn, w_ref[:, sl], preferred_element_type=F32).astype(o_ref.dtype)


def norm_matmul(x, g, w, layer, out_dtype, *, tm, tn, name):
    m, k = x.shape
    n = w.shape[2]
    assert m % tm == 0 and n % tn == 0
    return pl.pallas_call(
        _norm_matmul_kernel,
        out_shape=jax.ShapeDtypeStruct((m, n), out_dtype),
        grid=(m // tm, n // tn),
        in_specs=[
            pl.BlockSpec((tm, k), lambda i, j: (i, 0)),
            pl.BlockSpec((1, k), lambda i, j: (0, 0)),
            pl.BlockSpec((None, k, tn), lambda i, j: (layer, 0, j)),
        ],
        out_specs=pl.BlockSpec((tm, tn), lambda i, j: (i, j)),
        scratch_shapes=[pltpu.VMEM((tm, k), BF16)],
        compiler_params=_cparams("parallel", "arbitrary"),
        name=name,
    )(x, g.reshape(1, k), w)


def _sb_inproj_kernel(x_ref, g_ref, w_ref, hg_ref, o_ref, xn_ref):
    _rms_to(xn_ref, x_ref, g_ref)
    xn = xn_ref[...]
    d = o_ref.shape[1] // 3
    for sl in _col_chunks(3 * d):
        y = jnp.dot(xn, w_ref[:, sl], preferred_element_type=F32)
        region = sl.start // d
        if region == 2:
            o_ref[:, sl] = y.astype(o_ref.dtype)
            continue
        scale = SB_HEAD_DIM ** -0.5 if region == 0 else 1.0
        for h in range((sl.stop - sl.start) // SB_HEAD_DIM):
            blk = y[:, h * SB_HEAD_DIM:(h + 1) * SB_HEAD_DIM]
            ms = jnp.mean(blk * blk, axis=-1, keepdims=True)
            nb = blk * lax.rsqrt(ms + EPS) * hg_ref[region]
            cols = slice(sl.start + h * SB_HEAD_DIM, sl.start + (h + 1) * SB_HEAD_DIM)
            o_ref[:, cols] = (nb * scale).astype(o_ref.dtype)


def sb_inproj(x, g, w, layer, q_gain, k_gain, *, tm=512):
    m, k = x.shape
    n = w.shape[2]
    hg = jnp.stack([q_gain, k_gain]).reshape(2, 1, SB_HEAD_DIM)
    return pl.pallas_call(
        _sb_inproj_kernel,
        out_shape=jax.ShapeDtypeStruct((m, n), BF16),
        grid=(m // tm,),
        in_specs=[
            pl.BlockSpec((tm, k), lambda i: (i, 0)),
            _const_spec((1, k)),
            _const_spec((k, n), layer),
            _const_spec((2, 1, SB_HEAD_DIM)),
        ],
        out_specs=pl.BlockSpec((tm, n), lambda i: (i, 0)),
        scratch_shapes=[pltpu.VMEM((tm, k), BF16)],
        compiler_params=_cparams("parallel"),
        name="sb_inproj",
    )(x, g.reshape(1, k), w, hg)


def _sb_attn_kernel(q_ref, k_ref, v_ref, suf_ref, *rest, blk, nsub, n_cast):
    cast_in, o_ref, cast_out = rest[:n_cast], rest[n_cast], rest[n_cast + 1:]
    _side_cast(cast_in, cast_out)
    qi = pl.program_id(1)
    row = lax.broadcasted_iota(jnp.int32, (blk, blk), 0)
    col = lax.broadcasted_iota(jnp.int32, (blk, blk), 1)
    strict = col < row
    suffix = suf_ref[...]

    subs = range(nsub)

    def suffix_sums(lgs):
        his = [lg.astype(BF16) for lg in lgs]
        los = [(lg - hi.astype(F32)).astype(BF16) for lg, hi in zip(lgs, his)]
        return [jnp.dot(hi, suffix, preferred_element_type=F32)
                + jnp.dot(lo, suffix, preferred_element_type=F32) for hi, lo in zip(his, los)]

    def scores(q, kb):
        r = pl.multiple_of(kb * blk, blk)
        k = k_ref[pl.ds(r, blk), :]
        z = lax.dot_general(q, k, (((1,), (1,)), ((), ())), preferred_element_type=F32)
        return z * LOG2_E, v_ref[pl.ds(r, blk), :]

    def earlier_blocks(kbs, carries):
        zv = [scores(qs[j], jnp.maximum(kbs[j], 0)) for j in subs]
        cs = suffix_sums([_softplus_log2(zz) for zz, _ in zv])
        a = [jnp.where(kbs[j] >= 0, jnp.exp2(zv[j][0] - cs[j] - carries[j]), 0.0) for j in subs]
        new_c = [carries[j] + jnp.where(kbs[j] >= 0, cs[j][:, 0:1], 0.0) for j in subs]
        pv = [jnp.dot(a[j].astype(BF16), zv[j][1], preferred_element_type=F32) for j in subs]
        return new_c, pv

    qs = [q_ref[j * blk:(j + 1) * blk, :] for j in subs]
    gbs = [qi * nsub + j for j in subs]
    has_prev = [gb >= 1 for gb in gbs]
    zv_o = [scores(qs[j], gbs[j]) for j in subs]
    zv_p = [scores(qs[j], jnp.maximum(gbs[j] - 1, 0)) for j in subs]
    sp_o = [jnp.where(strict, _softplus_log2(zz), 0.0) for zz, _ in zv_o]
    cs_o = suffix_sums(sp_o)
    sp_p = [_softplus_log2(zz) for zz, _ in zv_p]
    cs_p = suffix_sums(sp_p)
    a_o = [jnp.where(strict, jnp.exp2(zv_o[j][0] - cs_o[j]), 0.0) for j in subs]
    acc_o = [jnp.dot(a_o[j].astype(BF16), zv_o[j][1], preferred_element_type=F32) for j in subs]
    c_o = [c[:, 0:1] for c in cs_o]
    a_p = [jnp.where(has_prev[j], jnp.exp2(zv_p[j][0] - cs_p[j] - c_o[j]), 0.0) for j in subs]
    acc_p = [jnp.dot(a_p[j].astype(BF16), zv_p[j][1], preferred_element_type=F32) for j in subs]
    carries = [c_o[j] + jnp.where(has_prev[j], cs_p[j][:, 0:1], 0.0) for j in subs]
    accs = [acc_o[j] + acc_p[j] for j in subs]

    n_more = qi * nsub + nsub - 2

    def cond(state):
        i, carries, _ = state
        live = carries[0]
        for c in carries[1:]:
            live = jnp.minimum(live, c)
        return jnp.logical_and(i < n_more, jnp.min(live) < EXP2_ZERO_ABOVE)

    def body(state):
        i, carries, accs = state
        new_c, pv = earlier_blocks([gbs[j] - 2 - i for j in subs], carries)
        return i + 1, tuple(new_c), tuple(accs[j] + pv[j] for j in subs)

    _, _, accs = lax.while_loop(cond, body, (jnp.int32(0), tuple(carries), tuple(accs)))
    for j in subs:
        o_ref[j * blk:(j + 1) * blk, :] = accs[j].astype(o_ref.dtype)


def sb_attention(qkv, cast_stacks=(), cast_layer=0, *, blk=256, nsub=4):
    seq, n3 = qkv.shape
    d = n3 // 3
    heads = d // SB_HEAD_DIM
    tq = blk * nsub
    assert seq % tq == 0
    nq = seq // tq
    idx = jnp.arange(blk)
    suffix = (idx[:, None] >= idx[None, :]).astype(BF16)
    c_in, c_out, c_shapes = _side_cast_plan(cast_stacks, cast_layer, heads * nq, lambda h, i: h * nq + i)
    kern = functools.partial(_sb_attn_kernel, blk=blk, nsub=nsub, n_cast=len(cast_stacks))
    return pl.pallas_call(
        kern,
        out_shape=[jax.ShapeDtypeStruct((seq, d), BF16)] + c_shapes,
        grid=(heads, nq),
        in_specs=[
            pl.BlockSpec((tq, SB_HEAD_DIM), lambda h, i: (i, h)),
            pl.BlockSpec((seq, SB_HEAD_DIM), lambda h, i: (0, heads + h)),
            pl.BlockSpec((seq, SB_HEAD_DIM), lambda h, i: (0, 2 * heads + h)),
            _const_spec((blk, blk)),
        ] + c_in,
        out_specs=[pl.BlockSpec((tq, SB_HEAD_DIM), lambda h, i: (i, h))] + c_out,
        compiler_params=_cparams("arbitrary", "arbitrary"),
        name="sb_attention",
    )(qkv, qkv, qkv, suffix, *cast_stacks)


def _resid_matmul_kernel(x_ref, a_ref, w_ref, o_ref):
    a = a_ref[...]
    for sl in _col_chunks(o_ref.shape[1]):
        o_ref[:, sl] = x_ref[:, sl] + jnp.dot(a, w_ref[:, sl], preferred_element_type=F32)


def resid_matmul(x, a, w, layer, *, tm=512, name="resid_matmul"):
    m, n = x.shape
    k = a.shape[1]
    return pl.pallas_call(
        _resid_matmul_kernel,
        out_shape=jax.ShapeDtypeStruct((m, n), F32),
        grid=(m // tm,),
        in_specs=[
            pl.BlockSpec((tm, n), lambda i: (i, 0)),
            pl.BlockSpec((tm, k), lambda i: (i, 0)),
            _const_spec((k, n), layer),
        ],
        out_specs=pl.BlockSpec((tm, n), lambda i: (i, 0)),
        compiler_params=_cparams("parallel"),
        name=name,
    )(x, a, w)


def _ffn_kernel(x_ref, g_ref, wg_ref, wu_ref, wd_ref, o_ref, xn_ref):
    @pl.when(pl.program_id(1) == 0)
    def _():
        _rms_to(xn_ref, x_ref, g_ref, copy_ref=o_ref)

    xn = xn_ref[...]
    gate = jnp.dot(xn, wg_ref[...], preferred_element_type=F32)
    up = jnp.dot(xn, wu_ref[...], preferred_element_type=F32)
    hid = (gate * _sigmoid(gate) * up).astype(BF16)
    o_ref[...] += jnp.dot(hid, wd_ref[...], preferred_element_type=F32)


def ffn(x, g, w_gate, w_up, w_down, *, tm=1024, tf=512):
    m, d = x.shape
    dff = w_gate.shape[1]
    assert m % tm == 0 and dff % tf == 0
    return pl.pallas_call(
        _ffn_kernel,
        out_shape=jax.ShapeDtypeStruct((m, d), F32),
        grid=(m // tm, dff // tf),
        in_specs=[
            pl.BlockSpec((tm, d), lambda i, f: (i, 0)),
            pl.BlockSpec((1, d), lambda i, f: (0, 0)),
            pl.BlockSpec((d, tf), lambda i, f: (0, f)),
            pl.BlockSpec((d, tf), lambda i, f: (0, f)),
            pl.BlockSpec((tf, d), lambda i, f: (f, 0)),
        ],
        out_specs=pl.BlockSpec((tm, d), lambda i, f: (i, 0)),
        scratch_shapes=[pltpu.VMEM((tm, d), BF16)],
        compiler_params=_cparams("parallel", "arbitrary"),
        name="ffn",
    )(x, g.reshape(1, d), w_gate, w_up, w_down)


def _ple_kernel(x_ref, g_ref, p_ref, wg_ref, wp_ref, o_ref, xn_ref):
    _rms_to(xn_ref, x_ref, g_ref)
    xn = xn_ref[...]
    pb = p_ref[...].astype(BF16)
    for sl in _col_chunks(o_ref.shape[1]):
        gate = _sigmoid(jnp.dot(xn, wg_ref[:, sl], preferred_element_type=F32))
        proj = jnp.dot(pb, wp_ref[:, sl], preferred_element_type=F32)
        o_ref[:, sl] = x_ref[:, sl] + gate * proj


def ple(x, g, p, w_gate, w_proj, layer, batch, *, tm=512):
    m, d = x.shape
    pd = p.shape[3]
    return pl.pallas_call(
        _ple_kernel,
        out_shape=jax.ShapeDtypeStruct((m, d), F32),
        grid=(m // tm,),
        in_specs=[
            pl.BlockSpec((tm, d), lambda i: (i, 0)),
            _const_spec((1, d)),
            pl.BlockSpec((None, None, tm, pd), lambda i: (layer, batch, i, 0)),
            _const_spec((d, d), layer),
            _const_spec((pd, d), layer),
        ],
        out_specs=pl.BlockSpec((tm, d), lambda i: (i, 0)),
        scratch_shapes=[pltpu.VMEM((tm, d), BF16)],
        compiler_params=_cparams("parallel"),
        name="ple",
    )(x, g.reshape(1, d), p, w_gate, w_proj)


def _s5_disc(lr, li, dt):
    mag = jnp.exp(lr * dt)
    ar = mag * jnp.cos(li * dt)
    ai = mag * jnp.sin(li * dt)
    den = lr * lr + li * li
    cr = ((ar - 1.0) * lr + ai * li) / den
    ci = (ai * lr - (ar - 1.0) * li) / den
    return ar, ai, cr, ci


def _s5_params_kernel(lr_ref, li_ref, ldt_ref, lrx_ref, lix_ref, ldtx_ref, bre_ref, bim_ref,
                      ar_ref, ai_ref, bbr_ref, bbi_ref):
    ar, ai, _, _ = _s5_disc(lr_ref[...], li_ref[...], jnp.exp(ldt_ref[...]))
    ar_ref[...] = ar
    ai_ref[...] = ai
    _, _, cr, ci = _s5_disc(lrx_ref[...], lix_ref[...], jnp.exp(ldtx_ref[...]))
    br = bre_ref[...]
    bi = bim_ref[...]
    bbr_ref[...] = cr * br - ci * bi
    bbi_ref[...] = cr * bi + ci * br


def s5_params(lam_re, lam_im, log_dt, b_re, b_im):
    g, p = lam_re.shape
    gs = b_re.shape[-1]
    ldt = jnp.broadcast_to(log_dt[:, None], (g, p))
    rep = lambda t: jnp.repeat(t, gs, axis=1)
    shp = jax.ShapeDtypeStruct
    return pl.pallas_call(
        _s5_params_kernel,
        out_shape=(shp((g, p), F32), shp((g, p), F32), shp((g, p * gs), F32), shp((g, p * gs), F32)),
        name="s5_params",
    )(lam_re, lam_im, ldt, rep(lam_re), rep(lam_im), rep(ldt),
      b_re.reshape(g, p * gs), b_im.reshape(g, p * gs))


S5_NBLK = 8
S5_T = 128
S5_PITCH = S5_T + 4


def _gelu_tanh(y):
    return 0.5 * y * (1.0 + jnp.tanh(math.sqrt(2.0 / math.pi) * (y + 0.044715 * (y * y * y))))


def _s5_scan_kernel(u_ref, bre_ref, bim_ref, cre_ref, cim_ref, ar_ref, ai_ref, d_ref, *rest, n_cast):
    cast_in, o_ref, cast_out = rest[:n_cast], rest[n_cast], rest[n_cast + 1:2 * n_cast + 1]
    br_ref, bi_ref, xr_ref, xi_ref, sr_ref, si_ref = rest[2 * n_cast + 1:]
    _side_cast(cast_in, cast_out)
    nslab = xr_ref.shape[0]
    ucols = u_ref.shape[1] // S5_NBLK

    @pl.when(pl.program_id(0) == 0)
    def _():
        sr_ref[...] = jnp.zeros_like(sr_ref)
        si_ref[...] = jnp.zeros_like(si_ref)

    for kb in range(S5_NBLK):
        ukb = u_ref[:, kb * ucols:(kb + 1) * ucols].astype(BF16)
        bur = jnp.dot(ukb, bre_ref[kb], preferred_element_type=F32)
        bui = jnp.dot(ukb, bim_ref[kb], preferred_element_type=F32)
        for cs in range(nslab):
            br_ref[cs, kb * S5_PITCH:kb * S5_PITCH + S5_T, :] = bur[:, cs * LANES:(cs + 1) * LANES]
            bi_ref[cs, kb * S5_PITCH:kb * S5_PITCH + S5_T, :] = bui[:, cs * LANES:(cs + 1) * LANES]

    def step(t, carry):
        new = []
        for cs in range(nslab):
            sr, si = carry[cs]
            ar = ar_ref[:, cs * LANES:(cs + 1) * LANES]
            ai = ai_ref[:, cs * LANES:(cs + 1) * LANES]
            rows = pl.ds(t, S5_NBLK, stride=S5_PITCH)
            nr = ar * sr - ai * si + br_ref[cs, rows, :]
            ni = ar * si + ai * sr + bi_ref[cs, rows, :]
            xr_ref[cs, rows, :] = nr
            xi_ref[cs, rows, :] = ni
            new.append((nr, ni))
        return tuple(new)

    init = tuple((sr_ref[:, cs * LANES:(cs + 1) * LANES], si_ref[:, cs * LANES:(cs + 1) * LANES])
                 for cs in range(nslab))
    final = lax.fori_loop(0, S5_T, step, init, unroll=2)
    for cs in range(nslab):
        sr_ref[:, cs * LANES:(cs + 1) * LANES] = final[cs][0]
        si_ref[:, cs * LANES:(cs + 1) * LANES] = final[cs][1]

    for kb in range(S5_NBLK):
        rows = slice(kb * S5_PITCH, kb * S5_PITCH + S5_T)
        xr = jnp.concatenate([xr_ref[cs, rows, :] for cs in range(nslab)], axis=1).astype(BF16)
        xi = jnp.concatenate([xi_ref[cs, rows, :] for cs in range(nslab)], axis=1).astype(BF16)
        y = (jnp.dot(xr, cre_ref[kb], preferred_element_type=F32)
             - jnp.dot(xi, cim_ref[kb], preferred_element_type=F32))
        sl = slice(kb * ucols, (kb + 1) * ucols)
        y = y + d_ref[:, sl] * u_ref[:, sl]
        o_ref[:, sl] = _gelu_tanh(y)


def s5_scan(u, bblk_re, bblk_im, cblk_re, cblk_im, a_re, a_im, d_skip, cast_stacks=(), cast_layer=0):
    seq, d = u.shape
    nstate = a_re.size
    blk_states = nstate // S5_NBLK
    nslab = blk_states // LANES
    ucols = d // S5_NBLK
    nc = seq // S5_T
    c_in, c_out, c_shapes = _side_cast_plan(cast_stacks, cast_layer, nc, lambda c: c)
    return pl.pallas_call(
        functools.partial(_s5_scan_kernel, n_cast=len(cast_stacks)),
        out_shape=[jax.ShapeDtypeStruct((seq, d), F32)] + c_shapes,
        grid=(nc,),
        in_specs=[
            pl.BlockSpec((S5_T, d), lambda c: (c, 0)),
            _const_spec((S5_NBLK, ucols, blk_states)),
            _const_spec((S5_NBLK, ucols, blk_states)),
            _const_spec((S5_NBLK, blk_states, ucols)),
            _const_spec((S5_NBLK, blk_states, ucols)),
            _const_spec((S5_NBLK, blk_states)),
            _const_spec((S5_NBLK, blk_states)),
            _const_spec((1, d)),
        ] + c_in,
        out_specs=[pl.BlockSpec((S5_T, d), lambda c: (c, 0))] + c_out,
        scratch_shapes=[
            pltpu.VMEM((nslab, S5_NBLK * S5_PITCH, LANES), F32),
            pltpu.VMEM((nslab, S5_NBLK * S5_PITCH, LANES), F32),
            pltpu.VMEM((nslab, S5_NBLK * S5_PITCH, LANES), F32),
            pltpu.VMEM((nslab, S5_NBLK * S5_PITCH, LANES), F32),
            pltpu.VMEM((S5_NBLK, blk_states), F32),
            pltpu.VMEM((S5_NBLK, blk_states), F32),
        ],
        compiler_params=_cparams("arbitrary"),
        name="s5_scan",
    )(u, bblk_re, bblk_im, cblk_re, cblk_im,
      a_re.reshape(S5_NBLK, blk_states), a_im.reshape(S5_NBLK, blk_states), d_skip.reshape(1, d),
      *cast_stacks)


def _glu_out_kernel(x_ref, g_ref, wglu_ref, wout_ref, o_ref, h_ref):
    gb = g_ref[...].astype(BF16)
    chunks = _col_chunks(o_ref.shape[1])
    for sl in chunks:
        gate = _sigmoid(jnp.dot(gb, wglu_ref[:, sl], preferred_element_type=F32))
        h_ref[:, sl] = (g_ref[:, sl] * gate).astype(BF16)
    h = h_ref[...]
    for sl in chunks:
        o_ref[:, sl] = x_ref[:, sl] + jnp.dot(h, wout_ref[:, sl], preferred_element_type=F32)


def glu_out(x, g, w_glu, w_out, layer, *, tm=512):
    m, d = x.shape
    return pl.pallas_call(
        _glu_out_kernel,
        out_shape=jax.ShapeDtypeStruct((m, d), F32),
        grid=(m // tm,),
        in_specs=[
            pl.BlockSpec((tm, d), lambda i: (i, 0)),
            pl.BlockSpec((tm, d), lambda i: (i, 0)),
            _const_spec((d, d), layer),
            _const_spec((d, d), layer),
        ],
        out_specs=pl.BlockSpec((tm, d), lambda i: (i, 0)),
        scratch_shapes=[pltpu.VMEM((tm, d), BF16)],
        compiler_params=_cparams("parallel"),
        name="s5_glu_out",
    )(x, g, w_glu, w_out)


def _ml_inproj_kernel(x_ref, g_ref, w_ref, wg_ref, qkv_ref, o_ref, gates_ref, xn_ref):
    _rms_to(xn_ref, x_ref, g_ref)
    xn = xn_ref[...]
    n_qkv = qkv_ref.shape[1]
    gates_ref[...] = jnp.dot(xn, wg_ref[...], preferred_element_type=F32)
    for sl in _col_chunks(n_qkv):
        qkv_ref[:, sl] = jnp.dot(xn, w_ref[:, sl], preferred_element_type=F32).astype(qkv_ref.dtype)
    for sl in _col_chunks(o_ref.shape[1]):
        wcols = slice(n_qkv + sl.start, n_qkv + sl.stop)
        o_ref[:, sl] = jnp.dot(xn, w_ref[:, wcols], preferred_element_type=F32)


def ml_inproj(x, g, w, layer, w_gates, n_qkv, n_o, *, tm=512):
    m, k = x.shape
    shp = jax.ShapeDtypeStruct
    return pl.pallas_call(
        _ml_inproj_kernel,
        out_shape=(shp((m, n_qkv), BF16), shp((m, n_o), F32), shp((m, LANES), F32)),
        grid=(m // tm,),
        in_specs=[
            pl.BlockSpec((tm, k), lambda i: (i, 0)),
            _const_spec((1, k)),
            _const_spec((k, w.shape[2]), layer),
            _const_spec((k, LANES)),
        ],
        out_specs=(
            pl.BlockSpec((tm, n_qkv), lambda i: (i, 0)),
            pl.BlockSpec((tm, n_o), lambda i: (i, 0)),
            pl.BlockSpec((tm, LANES), lambda i: (i, 0)),
        ),
        scratch_shapes=[pltpu.VMEM((tm, k), BF16)],
        compiler_params=_cparams("parallel"),
        name="ml_inproj",
    )(x, g.reshape(1, k), w, w_gates)


ML_T = 128


def _mlstm_kernel(q_ref, k_ref, v_ref, o_ref, gc_ref, gr_ref, bc_ref, br_ref, hg_ref, *rest, n_cast):
    cast_in, out_ref, cast_out = rest[:n_cast], rest[n_cast], rest[n_cast + 1:2 * n_cast + 1]
    c_ref, m_ref = rest[2 * n_cast + 1:]
    _side_cast(cast_in, cast_out)
    t = ML_T
    nh = ML_HEADS
    dqk = q_ref.shape[1] // nh
    dv = v_ref.shape[1] // nh
    kscale = dqk ** -0.5

    @pl.when(pl.program_id(0) == 0)
    def _():
        c_ref[...] = jnp.zeros_like(c_ref)
        m_ref[...] = jnp.zeros_like(m_ref)

    row = lax.broadcasted_iota(jnp.int32, (t, t), 0)
    col = lax.broadcasted_iota(jnp.int32, (t, t), 1)
    causal = col <= row
    lower = causal.astype(F32)
    upper = (row <= col).astype(F32)
    eye = (lax.broadcasted_iota(jnp.int32, (dqk, dqk), 0)
           == lax.broadcasted_iota(jnp.int32, (dqk, dqk), 1)).astype(BF16)
    ones_col = (lax.broadcasted_iota(jnp.int32, (t, LANES), 1) == 0).astype(BF16)

    gcol = gc_ref[...] + bc_ref[...]
    lf_col = _log_sigmoid(gcol)
    bcum_col = jnp.dot(lower, lf_col, preferred_element_type=F32, precision=lax.Precision.HIGHEST)
    grow = gr_ref[...] + br_ref[...]
    lf_row = _log_sigmoid(grow)
    bcum_row = jnp.dot(lf_row, upper, preferred_element_type=F32, precision=lax.Precision.HIGHEST)

    m_all = m_ref[...]
    heads = range(nh)
    nt = (((1,), (1,)), ((), ()))
    qs = [q_ref[:, h * dqk:(h + 1) * dqk] for h in heads]
    ks = [k_ref[:, h * dqk:(h + 1) * dqk] for h in heads]
    vs = [v_ref[:, h * dv:(h + 1) * dv] for h in heads]
    cs = [c_ref[h] for h in heads]
    bcs = [bcum_col[:, nh + h:nh + h + 1] for h in heads]
    brs = [bcum_row[nh + h:nh + h + 1, :] for h in heads]
    irs = [grow[h:h + 1, :] for h in heads]
    ms_ = [m_all[h:h + 1, 0:1] for h in heads]

    qk = [lax.dot_general(qs[h], ks[h], nt, preferred_element_type=F32) for h in heads]
    qc = [jnp.dot(qs[h], cs[h].astype(BF16), preferred_element_type=F32) for h in heads]
    kt = [lax.dot_general(eye, ks[h], nt, preferred_element_type=F32) for h in heads]

    dmat = [jnp.where(causal, bcs[h] - brs[h] + irs[h], -jnp.inf) for h in heads]
    inter = [bcs[h] + ms_[h] for h in heads]
    m_row = [jnp.maximum(jnp.max(dmat[h], axis=1, keepdims=True), inter[h]) for h in heads]
    s = [qk[h] * (jnp.exp(dmat[h] - m_row[h]) * kscale) for h in heads]
    w_inter = [jnp.exp(inter[h] - m_row[h]) for h in heads]
    sv = [jnp.dot(s[h].astype(BF16), vs[h], preferred_element_type=F32) for h in heads]

    b_last = [brs[h][:, t - 1:t] for h in heads]
    dec = [b_last[h] - brs[h] + irs[h] for h in heads]
    m_new = [jnp.maximum(b_last[h] + ms_[h], jnp.max(dec[h], axis=1, keepdims=True)) for h in heads]
    wk = [jnp.exp(dec[h] - m_new[h]) * kscale for h in heads]
    carry_scale = [jnp.exp(b_last[h] + ms_[h] - m_new[h]) for h in heads]
    for h in heads:
        v_ext = jnp.concatenate([vs[h], ones_col], axis=1)
        c_ref[h] = carry_scale[h] * cs[h] + jnp.dot((kt[h] * wk[h]).astype(BF16), v_ext,
                                                    preferred_element_type=F32)
    m_ref[...] = jnp.concatenate([jnp.broadcast_to(m_new[h], (1, m_ref.shape[1])) for h in heads],
                                 axis=0)

    num = [sv[h] + w_inter[h] * qc[h][:, :dv] for h in heads]
    den = [jnp.sum(s[h], axis=1, keepdims=True) + w_inter[h] * qc[h][:, dv:dv + 1] for h in heads]
    h_t = [num[h] / jnp.maximum(jnp.abs(den[h]), jnp.exp(-m_row[h])) for h in heads]
    msq = [jnp.mean(h_t[h] * h_t[h], axis=1, keepdims=True) for h in heads]
    for h in heads:
        hn = h_t[h] * lax.rsqrt(msq[h] + EPS) * hg_ref[:, h * dv:(h + 1) * dv]
        og = _sigmoid(o_ref[:, h * dv:(h + 1) * dv])
        out_ref[:, h * dv:(h + 1) * dv] = (og * hn).astype(out_ref.dtype)


def mlstm_scan(qkv, o, g_col, g_row, bias, head_gain, cast_stacks=(), cast_layer=0):
    seq = qkv.shape[0]
    d = o.shape[1]
    nqk = (qkv.shape[1] - d) // 2
    nh = ML_HEADS
    assert d == 2 * nqk
    bias_col = jnp.zeros((1, LANES), F32).at[0, :2 * nh].set(bias)
    bias_row = bias.reshape(2 * nh, 1)
    nc = seq // ML_T
    c_in, c_out, c_shapes = _side_cast_plan(cast_stacks, cast_layer, nc, lambda c: c)
    return pl.pallas_call(
        functools.partial(_mlstm_kernel, n_cast=len(cast_stacks)),
        out_shape=[jax.ShapeDtypeStruct((seq, d), BF16)] + c_shapes,
        grid=(nc,),
        in_specs=[
            pl.BlockSpec((ML_T, nqk), lambda c: (c, 0)),
            pl.BlockSpec((ML_T, nqk), lambda c: (c, 1)),
            pl.BlockSpec((ML_T, d), lambda c: (c, 1)),
            pl.BlockSpec((ML_T, d), lambda c: (c, 0)),
            pl.BlockSpec((ML_T, LANES), lambda c: (c, 0)),
            pl.BlockSpec((2 * nh, ML_T), lambda c: (0, c)),
            _const_spec((1, LANES)),
            _const_spec((2 * nh, 1)),
            _const_spec((1, d)),
        ] + c_in,
        out_specs=[pl.BlockSpec((ML_T, d), lambda c: (c, 0))] + c_out,
        scratch_shapes=[
            pltpu.VMEM((nh, nqk // nh, d // nh + LANES), F32),
            pltpu.VMEM((nh, LANES), F32),
        ],
        compiler_params=_cparams("arbitrary"),
        name="mlstm_scan",
    )(qkv, qkv, qkv, o, g_col, g_row, bias_col, bias_row, head_gain.reshape(1, d), *cast_stacks)


def _sb_layer(x, g, w_in, j, q_gain, k_gain, w_out, ffn_stacks, i):
    qkv = sb_inproj(x, g, w_in, j, q_gain, k_gain)
    o, *ffn_w = sb_attention(qkv, ffn_stacks, i)
    return resid_matmul(x, o, w_out, j, name="sb_out"), ffn_w


def _block_diag(t):
    nb, gb, r, c = t.shape
    eye = jnp.eye(gb, dtype=t.dtype)
    return (t[:, :, :, None, :] * eye[None, :, None, :, None]).reshape(nb, gb * r, gb * c)


def _s5_layer(x, g, w_in, j, lam_re, lam_im, log_dt, b_re, b_im, c_re, c_im, d_skip, w_glu, w_out,
              ffn_stacks, i):
    ngrp, nst = lam_re.shape
    gs = b_re.shape[-1]
    gb = ngrp // S5_NBLK
    d = x.shape[1]
    u = norm_matmul(x, g, w_in, j, F32, tm=512, tn=d, name="s5_in")
    a_re, a_im, bb_re, bb_im = s5_params(lam_re, lam_im, log_dt, b_re, b_im)
    to_b = lambda t: _block_diag(t.astype(BF16).reshape(S5_NBLK, gb, nst, gs).transpose(0, 1, 3, 2))
    to_c = lambda t: _block_diag(t.astype(BF16).reshape(S5_NBLK, gb, gs, nst).transpose(0, 1, 3, 2))
    y, *ffn_w = s5_scan(u, to_b(bb_re), to_b(bb_im), to_c(c_re), to_c(c_im), a_re, a_im, d_skip,
                        ffn_stacks, i)
    return glu_out(x, y, w_glu, w_out, j), ffn_w


def _mlstm_layer(x, g, w_in, j, gate_bias, head_gain, w_out, ffn_stacks, i):
    d = x.shape[1]
    nh = ML_HEADS
    nqk = d // 2
    n_qkv, n_o = 2 * nqk + d, d
    w_g = jnp.pad(w_in[j, :, n_qkv + n_o:], ((0, 0), (0, LANES - 2 * nh)))
    qkv, o, g_col = ml_inproj(x, g, w_in, j, w_g, n_qkv, n_o)
    g_row = g_col[:, :2 * nh].T
    hs, *ffn_w = mlstm_scan(qkv, o, g_col, g_row, gate_bias, head_gain, ffn_stacks, i)
    return resid_matmul(x, hs, w_out, j, name="ml_out"), ffn_w


def kernel(x, p, norm_mix, norm_ffn, sb_w_in, sb_q_gain, sb_k_gain, sb_w_out, s5_w_in, s5_lam_re, s5_lam_im, s5_log_dt, s5_b_re, s5_b_im, s5_c_re, s5_c_im, s5_d, s5_w_glu, s5_w_out, ml_w_in, ml_gate_bias, ml_head_gain, ml_w_out, ffn_w_gate, ffn_w_up, ffn_w_down, ple_norm, ple_w_gate, ple_w_proj):
    bsz, seq, d = x.shape
    depth = norm_mix.shape[0]
    bf = lambda t: t.astype(BF16)
    sb_w_in, sb_w_out, s5_w_in, s5_w_glu, s5_w_out = map(bf, (sb_w_in, sb_w_out, s5_w_in, s5_w_glu, s5_w_out))
    ml_w_in, ml_w_out, ple_w_gate, ple_w_proj = map(bf, (ml_w_in, ml_w_out, ple_w_gate, ple_w_proj))
    ffn_stacks = (ffn_w_gate, ffn_w_up, ffn_w_down)
    outs = []
    for b in range(bsz):
        h = x[b]
        for i in range(depth):
            kind, j = i % 3, i // 3
            if kind == 0:
                h, ffn_w = _sb_layer(h, norm_mix[i], sb_w_in, j, sb_q_gain[j], sb_k_gain[j], sb_w_out,
                                     ffn_stacks, i)
            elif kind == 1:
                h, ffn_w = _s5_layer(h, norm_mix[i], s5_w_in, j, s5_lam_re[j], s5_lam_im[j], s5_log_dt[j],
                                     s5_b_re[j], s5_b_im[j], s5_c_re[j], s5_c_im[j], s5_d[j],
                                     s5_w_glu, s5_w_out, ffn_stacks, i)
            else:
                h, ffn_w = _mlstm_layer(h, norm_mix[i], ml_w_in, j, ml_gate_bias[j], ml_head_gain[j],
                                        ml_w_out, ffn_stacks, i)
            h = ffn(h, norm_ffn[i], *ffn_w)
            h = ple(h, ple_norm[i], p, ple_w_gate, ple_w_proj, i, b)
        outs.append(h)
    return jnp.stack(outs)
```

```python
import functools
import math

import jax
import jax.numpy as jnp
from jax import lax
from jax.experimental import pallas as pl
from jax.experimental.pallas import tpu as pltpu

F32 = jnp.float32
BF16 = jnp.bfloat16

EPS = 1e-6
LANES = 128
SUBLANES = 8
VMEM_LIMIT = 56 * 1024 * 1024
COL_CHUNK = 512

SB_HEAD_DIM = 128
ML_HEADS = 8
EXP2_ZERO_ABOVE = 160.0


def _cparams(*sem):
    return pltpu.CompilerParams(dimension_semantics=sem, vmem_limit_bytes=VMEM_LIMIT)


def _const_spec(shape, layer=None):
    nd = len(shape)
    if layer is None:
        return pl.BlockSpec(shape, lambda *_: (0,) * nd, pipeline_mode=pl.Buffered(1))
    return pl.BlockSpec((None,) + tuple(shape), lambda *_: (layer,) + (0,) * nd,
                        pipeline_mode=pl.Buffered(1))


BF16_SUBLANES = 16


def _side_cast_plan(casts, n_steps, step_of):
    in_specs, out_specs, out_shapes = [], [], []
    for w, layer in casts:
        _, r, c = w.shape
        n_slabs = n_steps
        while r % (n_slabs * BF16_SUBLANES):
            n_slabs //= 2
        rows, per = r // n_slabs, n_steps // n_slabs
        in_specs.append(pl.BlockSpec((None, rows, c),
                                     lambda *g, per=per, layer=layer: (layer, step_of(*g) // per, 0)))
        out_specs.append(pl.BlockSpec((rows, c), lambda *g, per=per: (step_of(*g) // per, 0)))
        out_shapes.append(jax.ShapeDtypeStruct((r, c), BF16))
    return in_specs, out_specs, out_shapes


def _side_cast(in_refs, out_refs):
    for wi, wo in zip(in_refs, out_refs):
        wo[...] = wi[...].astype(wo.dtype)


def _rms_to(dst_ref, x_ref, g_ref, copy_ref=None, chunk=256):
    rows = x_ref.shape[0]
    chunk = min(chunk, rows)

    def body(c, _):
        r = pl.multiple_of(c * chunk, chunk)
        x = x_ref[pl.ds(r, chunk), :]
        ms = jnp.mean(x * x, axis=-1, keepdims=True)
        dst_ref[pl.ds(r, chunk), :] = (x * lax.rsqrt(ms + EPS) * g_ref[...]).astype(dst_ref.dtype)
        if copy_ref is not None:
            copy_ref[pl.ds(r, chunk), :] = x
        return 0

    lax.fori_loop(0, rows // chunk, body, 0)


def _col_chunks(n):
    c = min(COL_CHUNK, n)
    assert n % c == 0
    return [slice(i * c, (i + 1) * c) for i in range(n // c)]


def _log_sigmoid(z):
    return jnp.minimum(z, 0.0) - jnp.log1p(jnp.exp(-jnp.abs(z)))


LOG2_E = 1.4426950408889634


def _softplus_log2(zz):
    return jnp.maximum(zz, 0.0) + jnp.log2(1.0 + jnp.exp2(-jnp.abs(zz)))


def _sigmoid(z):
    return 1.0 / (1.0 + jnp.exp(-z))


def _norm_matmul_kernel(x_ref, g_ref, w_ref, o_ref, xn_ref):
    @pl.when(pl.program_id(1) == 0)
    def _():
        _rms_to(xn_ref, x_ref, g_ref)

    xn = xn_ref[...]
    for sl in _col_chunks(o_ref.shape[1]):
        o_ref[:, sl] = jnp.dot(xn, w_ref[:, sl], preferred_element_type=F32).astype(o_ref.dtype)


def norm_matmul(x, g, w, out_dtype, *, tm, tn, name):
    m, k = x.shape
    n = w.shape[1]
    assert m % tm == 0 and n % tn == 0
    return pl.pallas_call(
        _norm_matmul_kernel,
        out_shape=jax.ShapeDtypeStruct((m, n), out_dtype),
        grid=(m // tm, n // tn),
        in_specs=[
            pl.BlockSpec((tm, k), lambda i, j: (i, 0)),
            pl.BlockSpec((1, k), lambda i, j: (0, 0)),
            pl.BlockSpec((k, tn), lambda i, j: (0, j)),
        ],
        out_specs=pl.BlockSpec((tm, tn), lambda i, j: (i, j)),
        scratch_shapes=[pltpu.VMEM((tm, k), BF16)],
        compiler_params=_cparams("parallel", "arbitrary"),
        name=name,
    )(x, g.reshape(1, k), w)


def _sb_inproj_kernel(x_ref, g_ref, w_ref, hg_ref, o_ref, xn_ref):
    _rms_to(xn_ref, x_ref, g_ref)
    xn = xn_ref[...]
    d = o_ref.shape[1] // 3
    for sl in _col_chunks(3 * d):
        y = jnp.dot(xn, w_ref[:, sl], preferred_element_type=F32)
        region = sl.start // d
        if region == 2:
            o_ref[:, sl] = y.astype(o_ref.dtype)
            continue
        scale = SB_HEAD_DIM ** -0.5 if region == 0 else 1.0
        for h in range((sl.stop - sl.start) // SB_HEAD_DIM):
            blk = y[:, h * SB_HEAD_DIM:(h + 1) * SB_HEAD_DIM]
            ms = jnp.mean(blk * blk, axis=-1, keepdims=True)
            nb = blk * lax.rsqrt(ms + EPS) * hg_ref[region]
            cols = slice(sl.start + h * SB_HEAD_DIM, sl.start + (h + 1) * SB_HEAD_DIM)
            o_ref[:, cols] = (nb * scale).astype(o_ref.dtype)


def sb_inproj(x, g, w, q_gain, k_gain, *, tm=512):
    m, k = x.shape
    n = w.shape[1]
    hg = jnp.stack([q_gain, k_gain]).reshape(2, 1, SB_HEAD_DIM)
    return pl.pallas_call(
        _sb_inproj_kernel,
        out_shape=jax.ShapeDtypeStruct((m, n), BF16),
        grid=(m // tm,),
        in_specs=[
            pl.BlockSpec((tm, k), lambda i: (i, 0)),
            _const_spec((1, k)),
            _const_spec((k, n)),
            _const_spec((2, 1, SB_HEAD_DIM)),
        ],
        out_specs=pl.BlockSpec((tm, n), lambda i: (i, 0)),
        scratch_shapes=[pltpu.VMEM((tm, k), BF16)],
        compiler_params=_cparams("parallel"),
        name="sb_inproj",
    )(x, g.reshape(1, k), w, hg)


def _sb_attn_kernel(q_ref, k_ref, v_ref, suf_ref, *rest, blk, nsub, n_cast):
    cast_in, o_ref, cast_out = rest[:n_cast], rest[n_cast], rest[n_cast + 1:]
    _side_cast(cast_in, cast_out)
    qi = pl.program_id(1)
    row = lax.broadcasted_iota(jnp.int32, (blk, blk), 0)
    col = lax.broadcasted_iota(jnp.int32, (blk, blk), 1)
    strict = col < row
    suffix = suf_ref[...]

    subs = range(nsub)

    def suffix_sums(lgs):
        his = [lg.astype(BF16) for lg in lgs]
        los = [(lg - hi.astype(F32)).astype(BF16) for lg, hi in zip(lgs, his)]
        return [jnp.dot(hi, suffix, preferred_element_type=F32)
                + jnp.dot(lo, suffix, preferred_element_type=F32) for hi, lo in zip(his, los)]

    def scores(q, kb):
        r = pl.multiple_of(kb * blk, blk)
        k = k_ref[pl.ds(r, blk), :]
        z = lax.dot_general(q, k, (((1,), (1,)), ((), ())), preferred_element_type=F32)
        return z * LOG2_E, v_ref[pl.ds(r, blk), :]

    def earlier_blocks(kbs, carries):
        zv = [scores(qs[j], jnp.maximum(kbs[j], 0)) for j in subs]
        cs = suffix_sums([_softplus_log2(zz) for zz, _ in zv])
        a = [jnp.where(kbs[j] >= 0, jnp.exp2(zv[j][0] - cs[j] - carries[j]), 0.0) for j in subs]
        new_c = [carries[j] + jnp.where(kbs[j] >= 0, cs[j][:, 0:1], 0.0) for j in subs]
        pv = [jnp.dot(a[j].astype(BF16), zv[j][1], preferred_element_type=F32) for j in subs]
        return new_c, pv

    qs = [q_ref[j * blk:(j + 1) * blk, :] for j in subs]
    gbs = [qi * nsub + j for j in subs]
    has_prev = [gb >= 1 for gb in gbs]
    zv_o = [scores(qs[j], gbs[j]) for j in subs]
    zv_p = [scores(qs[j], jnp.maximum(gbs[j] - 1, 0)) for j in subs]
    sp_o = [jnp.where(strict, _softplus_log2(zz), 0.0) for zz, _ in zv_o]
    cs_o = suffix_sums(sp_o)
    sp_p = [_softplus_log2(zz) for zz, _ in zv_p]
    cs_p = suffix_sums(sp_p)
    a_o = [jnp.where(strict, jnp.exp2(zv_o[j][0] - cs_o[j]), 0.0) for j in subs]
    acc_o = [jnp.dot(a_o[j].astype(BF16), zv_o[j][1], preferred_element_type=F32) for j in subs]
    c_o = [c[:, 0:1] for c in cs_o]
    a_p = [jnp.where(has_prev[j], jnp.exp2(zv_p[j][0] - cs_p[j] - c_o[j]), 0.0) for j in subs]
    acc_p = [jnp.dot(a_p[j].astype(BF16), zv_p[j][1], preferred_element_type=F32) for j in subs]
    carries = [c_o[j] + jnp.where(has_prev[j], cs_p[j][:, 0:1], 0.0) for j in subs]
    accs = [acc_o[j] + acc_p[j] for j in subs]

    n_more = qi * nsub + nsub - 2

    def cond(state):
        i, carries, _ = state
        live = carries[0]
        for c in carries[1:]:
            live = jnp.minimum(live, c)
        return jnp.logical_and(i < n_more, jnp.min(live) < EXP2_ZERO_ABOVE)

    def body(state):
        i, carries, accs = state
        new_c, pv = earlier_blocks([gbs[j] - 2 - i for j in subs], carries)
        return i + 1, tuple(new_c), tuple(accs[j] + pv[j] for j in subs)

    _, _, accs = lax.while_loop(cond, body, (jnp.int32(0), tuple(carries), tuple(accs)))
    for j in subs:
        o_ref[j * blk:(j + 1) * blk, :] = accs[j].astype(o_ref.dtype)


def sb_attention(qkv, casts=(), *, blk=256, nsub=4):
    seq, n3 = qkv.shape
    d = n3 // 3
    heads = d // SB_HEAD_DIM
    tq = blk * nsub
    assert seq % tq == 0
    nq = seq // tq
    idx = jnp.arange(blk)
    suffix = (idx[:, None] >= idx[None, :]).astype(BF16)
    c_in, c_out, c_shapes = _side_cast_plan(casts, heads * nq, lambda h, i: h * nq + i)
    kern = functools.partial(_sb_attn_kernel, blk=blk, nsub=nsub, n_cast=len(casts))
    return pl.pallas_call(
        kern,
        out_shape=[jax.ShapeDtypeStruct((seq, d), BF16)] + c_shapes,
        grid=(heads, nq),
        in_specs=[
            pl.BlockSpec((tq, SB_HEAD_DIM), lambda h, i: (i, h)),
            pl.BlockSpec((seq, SB_HEAD_DIM), lambda h, i: (0, heads + h)),
            pl.BlockSpec((seq, SB_HEAD_DIM), lambda h, i: (0, 2 * heads + h)),
            _const_spec((blk, blk)),
        ] + c_in,
        out_specs=[pl.BlockSpec((tq, SB_HEAD_DIM), lambda h, i: (i, h))] + c_out,
        compiler_params=_cparams("arbitrary", "arbitrary"),
        name="sb_attention",
    )(qkv, qkv, qkv, suffix, *[w for w, _ in casts])


def _resid_matmul_kernel(x_ref, a_ref, w_ref, o_ref):
    a = a_ref[...]
    for sl in _col_chunks(o_ref.shape[1]):
        o_ref[:, sl] = x_ref[:, sl] + jnp.dot(a, w_ref[:, sl], preferred_element_type=F32)


def resid_matmul(x, a, w, *, tm=512, name="resid_matmul"):
    m, n = x.shape
    k = a.shape[1]
    return pl.pallas_call(
        _resid_matmul_kernel,
        out_shape=jax.ShapeDtypeStruct((m, n), F32),
        grid=(m // tm,),
        in_specs=[
            pl.BlockSpec((tm, n), lambda i: (i, 0)),
            pl.BlockSpec((tm, k), lambda i: (i, 0)),
            _const_spec((k, n)),
        ],
        out_specs=pl.BlockSpec((tm, n), lambda i: (i, 0)),
        compiler_params=_cparams("parallel"),
        name=name,
    )(x, a, w)


def _ffn_kernel(x_ref, g_ref, wg_ref, wu_ref, wd_ref, o_ref, xn_ref):
    @pl.when(pl.program_id(1) == 0)
    def _():
        _rms_to(xn_ref, x_ref, g_ref, copy_ref=o_ref)

    xn = xn_ref[...]
    gate = jnp.dot(xn, wg_ref[...], preferred_element_type=F32)
    up = jnp.dot(xn, wu_ref[...], preferred_element_type=F32)
    hid = (gate * _sigmoid(gate) * up).astype(BF16)
    o_ref[...] += jnp.dot(hid, wd_ref[...], preferred_element_type=F32)


def ffn(x, g, w_gate, w_up, w_down, *, tm=1024, tf=512):
    m, d = x.shape
    dff = w_gate.shape[1]
    assert m % tm == 0 and dff % tf == 0
    return pl.pallas_call(
        _ffn_kernel,
        out_shape=jax.ShapeDtypeStruct((m, d), F32),
        grid=(m // tm, dff // tf),
        in_specs=[
            pl.BlockSpec((tm, d), lambda i, f: (i, 0)),
            pl.BlockSpec((1, d), lambda i, f: (0, 0)),
            pl.BlockSpec((d, tf), lambda i, f: (0, f)),
            pl.BlockSpec((d, tf), lambda i, f: (0, f)),
            pl.BlockSpec((tf, d), lambda i, f: (f, 0)),
        ],
        out_specs=pl.BlockSpec((tm, d), lambda i, f: (i, 0)),
        scratch_shapes=[pltpu.VMEM((tm, d), BF16)],
        compiler_params=_cparams("parallel", "arbitrary"),
        name="ffn",
    )(x, g.reshape(1, d), w_gate, w_up, w_down)


def _ple_kernel(x_ref, g_ref, p_ref, wg_ref, wp_ref, o_ref, xn_ref):
    _rms_to(xn_ref, x_ref, g_ref)
    xn = xn_ref[...]
    pb = p_ref[...].astype(BF16)
    for sl in _col_chunks(o_ref.shape[1]):
        gate = _sigmoid(jnp.dot(xn, wg_ref[:, sl], preferred_element_type=F32))
        proj = jnp.dot(pb, wp_ref[:, sl], preferred_element_type=F32)
        o_ref[:, sl] = x_ref[:, sl] + gate * proj


def ple(x, g, p, w_gate, w_proj, layer, batch, *, tm=512):
    m, d = x.shape
    pd = p.shape[3]
    return pl.pallas_call(
        _ple_kernel,
        out_shape=jax.ShapeDtypeStruct((m, d), F32),
        grid=(m // tm,),
        in_specs=[
            pl.BlockSpec((tm, d), lambda i: (i, 0)),
            _const_spec((1, d)),
            pl.BlockSpec((None, None, tm, pd), lambda i: (layer, batch, i, 0)),
            _const_spec((d, d)),
            _const_spec((pd, d)),
        ],
        out_specs=pl.BlockSpec((tm, d), lambda i: (i, 0)),
        scratch_shapes=[pltpu.VMEM((tm, d), BF16)],
        compiler_params=_cparams("parallel"),
        name="ple",
    )(x, g.reshape(1, d), p, w_gate, w_proj)


def _s5_disc(lr, li, dt):
    mag = jnp.exp(lr * dt)
    ar = mag * jnp.cos(li * dt)
    ai = mag * jnp.sin(li * dt)
    den = lr * lr + li * li
    cr = ((ar - 1.0) * lr + ai * li) / den
    ci = (ai * lr - (ar - 1.0) * li) / den
    return ar, ai, cr, ci


def _s5_params_kernel(lr_ref, li_ref, ldt_ref, lrx_ref, lix_ref, ldtx_ref, bre_ref, bim_ref,
                      ar_ref, ai_ref, bbr_ref, bbi_ref):
    ar, ai, _, _ = _s5_disc(lr_ref[...], li_ref[...], jnp.exp(ldt_ref[...]))
    ar_ref[...] = ar
    ai_ref[...] = ai
    _, _, cr, ci = _s5_disc(lrx_ref[...], lix_ref[...], jnp.exp(ldtx_ref[...]))
    br = bre_ref[...]
    bi = bim_ref[...]
    bbr_ref[...] = cr * br - ci * bi
    bbi_ref[...] = cr * bi + ci * br


def s5_params(lam_re, lam_im, log_dt, b_re, b_im):
    g, p = lam_re.shape
    gs = b_re.shape[-1]
    ldt = jnp.broadcast_to(log_dt[:, None], (g, p))
    rep = lambda t: jnp.repeat(t, gs, axis=1)
    shp = jax.ShapeDtypeStruct
    return pl.pallas_call(
        _s5_params_kernel,
        out_shape=(shp((g, p), F32), shp((g, p), F32), shp((g, p * gs), F32), shp((g, p * gs), F32)),
        name="s5_params",
    )(lam_re, lam_im, ldt, rep(lam_re), rep(lam_im), rep(ldt),
      b_re.reshape(g, p * gs), b_im.reshape(g, p * gs))


S5_NBLK = 8
S5_T = 128
S5_PITCH = S5_T + 4


def _gelu_tanh(y):
    return 0.5 * y * (1.0 + jnp.tanh(math.sqrt(2.0 / math.pi) * (y + 0.044715 * (y * y * y))))


def _s5_scan_kernel(u_ref, bre_ref, bim_ref, cre_ref, cim_ref, ar_ref, ai_ref, d_ref, *rest, n_cast):
    cast_in, o_ref, cast_out = rest[:n_cast], rest[n_cast], rest[n_cast + 1:2 * n_cast + 1]
    br_ref, bi_ref, xr_ref, xi_ref, sr_ref, si_ref = rest[2 * n_cast + 1:]
    _side_cast(cast_in, cast_out)
    nslab = xr_ref.shape[0]
    ucols = u_ref.shape[1] // S5_NBLK

    @pl.when(pl.program_id(0) == 0)
    def _():
        sr_ref[...] = jnp.zeros_like(sr_ref)
        si_ref[...] = jnp.zeros_like(si_ref)

    for kb in range(S5_NBLK):
        ukb = u_ref[:, kb * ucols:(kb + 1) * ucols].astype(BF16)
        bur = jnp.dot(ukb, bre_ref[kb], preferred_element_type=F32)
        bui = jnp.dot(ukb, bim_ref[kb], preferred_element_type=F32)
        for cs in range(nslab):
            br_ref[cs, kb * S5_PITCH:kb * S5_PITCH + S5_T, :] = bur[:, cs * LANES:(cs + 1) * LANES]
            bi_ref[cs, kb * S5_PITCH:kb * S5_PITCH + S5_T, :] = bui[:, cs * LANES:(cs + 1) * LANES]

    def step(t, carry):
        new = []
        for cs in range(nslab):
            sr, si = carry[cs]
            ar = ar_ref[:, cs * LANES:(cs + 1) * LANES]
            ai = ai_ref[:, cs * LANES:(cs + 1) * LANES]
            rows = pl.ds(t, S5_NBLK, stride=S5_PITCH)
            nr = ar * sr - ai * si + br_ref[cs, rows, :]
            ni = ar * si + ai * sr + bi_ref[cs, rows, :]
            xr_ref[cs, rows, :] = nr
            xi_ref[cs, rows, :] = ni
            new.append((nr, ni))
        return tuple(new)

    init = tuple((sr_ref[:, cs * LANES:(cs + 1) * LANES], si_ref[:, cs * LANES:(cs + 1) * LANES])
                 for cs in range(nslab))
    final = lax.fori_loop(0, S5_T, step, init, unroll=2)
    for cs in range(nslab):
        sr_ref[:, cs * LANES:(cs + 1) * LANES] = final[cs][0]
        si_ref[:, cs * LANES:(cs + 1) * LANES] = final[cs][1]

    for kb in range(S5_NBLK):
        rows = slice(kb * S5_PITCH, kb * S5_PITCH + S5_T)
        xr = jnp.concatenate([xr_ref[cs, rows, :] for cs in range(nslab)], axis=1).astype(BF16)
        xi = jnp.concatenate([xi_ref[cs, rows, :] for cs in range(nslab)], axis=1).astype(BF16)
        y = (jnp.dot(xr, cre_ref[kb], preferred_element_type=F32)
             - jnp.dot(xi, cim_ref[kb], preferred_element_type=F32))
        sl = slice(kb * ucols, (kb + 1) * ucols)
        y = y + d_ref[:, sl] * u_ref[:, sl]
        o_ref[:, sl] = _gelu_tanh(y)


def s5_scan(u, bblk_re, bblk_im, cblk_re, cblk_im, a_re, a_im, d_skip, casts=()):
    seq, d = u.shape
    nstate = a_re.size
    blk_states = nstate // S5_NBLK
    nslab = blk_states // LANES
    ucols = d // S5_NBLK
    nc = seq // S5_T
    c_in, c_out, c_shapes = _side_cast_plan(casts, nc, lambda c: c)
    return pl.pallas_call(
        functools.partial(_s5_scan_kernel, n_cast=len(casts)),
        out_shape=[jax.ShapeDtypeStruct((seq, d), F32)] + c_shapes,
        grid=(nc,),
        in_specs=[
            pl.BlockSpec((S5_T, d), lambda c: (c, 0)),
            _const_spec((S5_NBLK, ucols, blk_states)),
            _const_spec((S5_NBLK, ucols, blk_states)),
            _const_spec((S5_NBLK, blk_states, ucols)),
            _const_spec((S5_NBLK, blk_states, ucols)),
            _const_spec((S5_NBLK, blk_states)),
            _const_spec((S5_NBLK, blk_states)),
            _const_spec((1, d)),
        ] + c_in,
        out_specs=[pl.BlockSpec((S5_T, d), lambda c: (c, 0))] + c_out,
        scratch_shapes=[
            pltpu.VMEM((nslab, S5_NBLK * S5_PITCH, LANES), F32),
            pltpu.VMEM((nslab, S5_NBLK * S5_PITCH, LANES), F32),
            pltpu.VMEM((nslab, S5_NBLK * S5_PITCH, LANES), F32),
            pltpu.VMEM((nslab, S5_NBLK * S5_PITCH, LANES), F32),
            pltpu.VMEM((S5_NBLK, blk_states), F32),
            pltpu.VMEM((S5_NBLK, blk_states), F32),
        ],
        compiler_params=_cparams("arbitrary"),
        name="s5_scan",
    )(u, bblk_re, bblk_im, cblk_re, cblk_im,
      a_re.reshape(S5_NBLK, blk_states), a_im.reshape(S5_NBLK, blk_states), d_skip.reshape(1, d),
      *[w for w, _ in casts])


def _glu_out_kernel(x_ref, g_ref, wglu_ref, wout_ref, o_ref, h_ref):
    gb = g_ref[...].astype(BF16)
    chunks = _col_chunks(o_ref.shape[1])
    for sl in chunks:
        gate = _sigmoid(jnp.dot(gb, wglu_ref[:, sl], preferred_element_type=F32))
        h_ref[:, sl] = (g_ref[:, sl] * gate).astype(BF16)
    h = h_ref[...]
    for sl in chunks:
        o_ref[:, sl] = x_ref[:, sl] + jnp.dot(h, wout_ref[:, sl], preferred_element_type=F32)


def glu_out(x, g, w_glu, w_out, *, tm=512):
    m, d = x.shape
    return pl.pallas_call(
        _glu_out_kernel,
        out_shape=jax.ShapeDtypeStruct((m, d), F32),
        grid=(m // tm,),
        in_specs=[
            pl.BlockSpec((tm, d), lambda i: (i, 0)),
            pl.BlockSpec((tm, d), lambda i: (i, 0)),
            _const_spec((d, d)),
            _const_spec((d, d)),
        ],
        out_specs=pl.BlockSpec((tm, d), lambda i: (i, 0)),
        scratch_shapes=[pltpu.VMEM((tm, d), BF16)],
        compiler_params=_cparams("parallel"),
        name="s5_glu_out",
    )(x, g, w_glu, w_out)


def _ml_inproj_kernel(x_ref, g_ref, w_ref, wg_ref, qkv_ref, o_ref, gates_ref, xn_ref):
    _rms_to(xn_ref, x_ref, g_ref)
    xn = xn_ref[...]
    n_qkv = qkv_ref.shape[1]
    gates_ref[...] = jnp.dot(xn, wg_ref[...], preferred_element_type=F32)
    for sl in _col_chunks(n_qkv):
        qkv_ref[:, sl] = jnp.dot(xn, w_ref[:, sl], preferred_element_type=F32).astype(qkv_ref.dtype)
    for sl in _col_chunks(o_ref.shape[1]):
        wcols = slice(n_qkv + sl.start, n_qkv + sl.stop)
        o_ref[:, sl] = jnp.dot(xn, w_ref[:, wcols], preferred_element_type=F32)


def ml_inproj(x, g, w, w_gates, n_qkv, n_o, *, tm=512):
    m, k = x.shape
    shp = jax.ShapeDtypeStruct
    return pl.pallas_call(
        _ml_inproj_kernel,
        out_shape=(shp((m, n_qkv), BF16), shp((m, n_o), F32), shp((m, LANES), F32)),
        grid=(m // tm,),
        in_specs=[
            pl.BlockSpec((tm, k), lambda i: (i, 0)),
            _const_spec((1, k)),
            _const_spec((k, w.shape[1])),
            _const_spec((k, LANES)),
        ],
        out_specs=(
            pl.BlockSpec((tm, n_qkv), lambda i: (i, 0)),
            pl.BlockSpec((tm, n_o), lambda i: (i, 0)),
            pl.BlockSpec((tm, LANES), lambda i: (i, 0)),
        ),
        scratch_shapes=[pltpu.VMEM((tm, k), BF16)],
        compiler_params=_cparams("parallel"),
        name="ml_inproj",
    )(x, g.reshape(1, k), w, w_gates)


ML_T = 128


def _mlstm_kernel(q_ref, k_ref, v_ref, o_ref, gc_ref, gr_ref, bc_ref, br_ref, hg_ref, *rest, n_cast):
    cast_in, out_ref, cast_out = rest[:n_cast], rest[n_cast], rest[n_cast + 1:2 * n_cast + 1]
    c_ref, m_ref = rest[2 * n_cast + 1:]
    _side_cast(cast_in, cast_out)
    t = ML_T
    nh = ML_HEADS
    dqk = q_ref.shape[1] // nh
    dv = v_ref.shape[1] // nh
    kscale = dqk ** -0.5

    @pl.when(pl.program_id(0) == 0)
    def _():
        c_ref[...] = jnp.zeros_like(c_ref)
        m_ref[...] = jnp.zeros_like(m_ref)

    row = lax.broadcasted_iota(jnp.int32, (t, t), 0)
    col = lax.broadcasted_iota(jnp.int32, (t, t), 1)
    causal = col <= row
    lower = causal.astype(F32)
    upper = (row <= col).astype(F32)
    eye = (lax.broadcasted_iota(jnp.int32, (dqk, dqk), 0)
           == lax.broadcasted_iota(jnp.int32, (dqk, dqk), 1)).astype(BF16)
    ones_col = (lax.broadcasted_iota(jnp.int32, (t, LANES), 1) == 0).astype(BF16)

    gcol = gc_ref[...] + bc_ref[...]
    lf_col = _log_sigmoid(gcol)
    bcum_col = jnp.dot(lower, lf_col, preferred_element_type=F32, precision=lax.Precision.HIGHEST)
    grow = gr_ref[...] + br_ref[...]
    lf_row = _log_sigmoid(grow)
    bcum_row = jnp.dot(lf_row, upper, preferred_element_type=F32, precision=lax.Precision.HIGHEST)

    m_all = m_ref[...]
    heads = range(nh)
    nt = (((1,), (1,)), ((), ()))
    qs = [q_ref[:, h * dqk:(h + 1) * dqk] for h in heads]
    ks = [k_ref[:, h * dqk:(h + 1) * dqk] for h in heads]
    vs = [v_ref[:, h * dv:(h + 1) * dv] for h in heads]
    cs = [c_ref[h] for h in heads]
    bcs = [bcum_col[:, nh + h:nh + h + 1] for h in heads]
    brs = [bcum_row[nh + h:nh + h + 1, :] for h in heads]
    irs = [grow[h:h + 1, :] for h in heads]
    ms_ = [m_all[h:h + 1, 0:1] for h in heads]

    qk = [lax.dot_general(qs[h], ks[h], nt, preferred_element_type=F32) for h in heads]
    qc = [jnp.dot(qs[h], cs[h].astype(BF16), preferred_element_type=F32) for h in heads]
    kt = [lax.dot_general(eye, ks[h], nt, preferred_element_type=F32) for h in heads]

    dmat = [jnp.where(causal, bcs[h] - brs[h] + irs[h], -jnp.inf) for h in heads]
    inter = [bcs[h] + ms_[h] for h in heads]
    m_row = [jnp.maximum(jnp.max(dmat[h], axis=1, keepdims=True), inter[h]) for h in heads]
    s = [qk[h] * (jnp.exp(dmat[h] - m_row[h]) * kscale) for h in heads]
    w_inter = [jnp.exp(inter[h] - m_row[h]) for h in heads]
    sv = [jnp.dot(s[h].astype(BF16), vs[h], preferred_element_type=F32) for h in heads]

    b_last = [brs[h][:, t - 1:t] for h in heads]
    dec = [b_last[h] - brs[h] + irs[h] for h in heads]
    m_new = [jnp.maximum(b_last[h] + ms_[h], jnp.max(dec[h], axis=1, keepdims=True)) for h in heads]
    wk = [jnp.exp(dec[h] - m_new[h]) * kscale for h in heads]
    carry_scale = [jnp.exp(b_last[h] + ms_[h] - m_new[h]) for h in heads]
    for h in heads:
        v_ext = jnp.concatenate([vs[h], ones_col], axis=1)
        c_ref[h] = carry_scale[h] * cs[h] + jnp.dot((kt[h] * wk[h]).astype(BF16), v_ext,
                                                    preferred_element_type=F32)
    m_ref[...] = jnp.concatenate([jnp.broadcast_to(m_new[h], (1, m_ref.shape[1])) for h in heads],
                                 axis=0)

    num = [sv[h] + w_inter[h] * qc[h][:, :dv] for h in heads]
    den = [jnp.sum(s[h], axis=1, keepdims=True) + w_inter[h] * qc[h][:, dv:dv + 1] for h in heads]
    h_t = [num[h] / jnp.maximum(jnp.abs(den[h]), jnp.exp(-m_row[h])) for h in heads]
    msq = [jnp.mean(h_t[h] * h_t[h], axis=1, keepdims=True) for h in heads]
    for h in heads:
        hn = h_t[h] * lax.rsqrt(msq[h] + EPS) * hg_ref[:, h * dv:(h + 1) * dv]
        og = _sigmoid(o_ref[:, h * dv:(h + 1) * dv])
        out_ref[:, h * dv:(h + 1) * dv] = (og * hn).astype(out_ref.dtype)


def mlstm_scan(qkv, o, g_col, g_row, bias, head_gain, casts=()):
    seq = qkv.shape[0]
    d = o.shape[1]
    nqk = (qkv.shape[1] - d) // 2
    nh = ML_HEADS
    assert d == 2 * nqk
    bias_col = jnp.zeros((1, LANES), F32).at[0, :2 * nh].set(bias)
    bias_row = bias.reshape(2 * nh, 1)
    nc = seq // ML_T
    c_in, c_out, c_shapes = _side_cast_plan(casts, nc, lambda c: c)
    return pl.pallas_call(
        functools.partial(_mlstm_kernel, n_cast=len(casts)),
        out_shape=[jax.ShapeDtypeStruct((seq, d), BF16)] + c_shapes,
        grid=(nc,),
        in_specs=[
            pl.BlockSpec((ML_T, nqk), lambda c: (c, 0)),
            pl.BlockSpec((ML_T, nqk), lambda c: (c, 1)),
            pl.BlockSpec((ML_T, d), lambda c: (c, 1)),
            pl.BlockSpec((ML_T, d), lambda c: (c, 0)),
            pl.BlockSpec((ML_T, LANES), lambda c: (c, 0)),
            pl.BlockSpec((2 * nh, ML_T), lambda c: (0, c)),
            _const_spec((1, LANES)),
            _const_spec((2 * nh, 1)),
            _const_spec((1, d)),
        ] + c_in,
        out_specs=[pl.BlockSpec((ML_T, d), lambda c: (c, 0))] + c_out,
        scratch_shapes=[
            pltpu.VMEM((nh, nqk // nh, d // nh + LANES), F32),
            pltpu.VMEM((nh, LANES), F32),
        ],
        compiler_params=_cparams("arbitrary"),
        name="mlstm_scan",
    )(qkv, qkv, qkv, o, g_col, g_row, bias_col, bias_row, head_gain.reshape(1, d),
      *[w for w, _ in casts])


def _sb_layer(x, g, w_in, q_gain, k_gain, w_out, later):
    qkv = sb_inproj(x, g, w_in, q_gain, k_gain)
    o, w_out_bf, *cast = sb_attention(qkv, [w_out] + later)
    return resid_matmul(x, o, w_out_bf, name="sb_out"), cast


def _block_diag(t):
    nb, gb, r, c = t.shape
    eye = jnp.eye(gb, dtype=t.dtype)
    return (t[:, :, :, None, :] * eye[None, :, None, :, None]).reshape(nb, gb * r, gb * c)


def _s5_layer(x, g, w_in, lam_re, lam_im, log_dt, b_re, b_im, c_re, c_im, d_skip, w_glu, w_out, later):
    ngrp, nst = lam_re.shape
    gs = b_re.shape[-1]
    gb = ngrp // S5_NBLK
    d = x.shape[1]
    u = norm_matmul(x, g, w_in, F32, tm=512, tn=d, name="s5_in")
    a_re, a_im, bb_re, bb_im = s5_params(lam_re, lam_im, log_dt, b_re, b_im)
    to_b = lambda t: _block_diag(t.astype(BF16).reshape(S5_NBLK, gb, nst, gs).transpose(0, 1, 3, 2))
    to_c = lambda t: _block_diag(t.astype(BF16).reshape(S5_NBLK, gb, gs, nst).transpose(0, 1, 3, 2))
    y, w_glu_bf, w_out_bf, *cast = s5_scan(u, to_b(bb_re), to_b(bb_im), to_c(c_re), to_c(c_im),
                                           a_re, a_im, d_skip, [w_glu, w_out] + later)
    return glu_out(x, y, w_glu_bf, w_out_bf), cast


def _mlstm_layer(x, g, w_in, gate_bias, head_gain, w_out, later):
    d = x.shape[1]
    nh = ML_HEADS
    nqk = d // 2
    n_qkv, n_o = 2 * nqk + d, d
    w_g = jnp.pad(w_in[:, n_qkv + n_o:], ((0, 0), (0, LANES - 2 * nh)))
    qkv, o, g_col = ml_inproj(x, g, w_in, w_g, n_qkv, n_o)
    g_row = g_col[:, :2 * nh].T
    hs, w_out_bf, *cast = mlstm_scan(qkv, o, g_col, g_row, gate_bias, head_gain, [w_out] + later)
    return resid_matmul(x, hs, w_out_bf, name="ml_out"), cast


def kernel(x, p, norm_mix, norm_ffn, sb_w_in, sb_q_gain, sb_k_gain, sb_w_out, s5_w_in, s5_lam_re, s5_lam_im, s5_log_dt, s5_b_re, s5_b_im, s5_c_re, s5_c_im, s5_d, s5_w_glu, s5_w_out, ml_w_in, ml_gate_bias, ml_head_gain, ml_w_out, ffn_w_gate, ffn_w_up, ffn_w_down, ple_norm, ple_w_gate, ple_w_proj):
    bsz, seq, d = x.shape
    depth = norm_mix.shape[0]

    def inproj(i):
        return ((sb_w_in, s5_w_in, ml_w_in)[i % 3], i // 3)

    def lane_aligned(w):
        return w.shape[-1] % LANES == 0

    outs = []
    for b in range(bsz):
        h = x[b]
        first, first_j = inproj(0)
        w_in = first[first_j].astype(BF16)
        for i in range(depth):
            kind, j = i % 3, i // 3
            later = [(ffn_w_gate, i), (ffn_w_up, i), (ffn_w_down, i), (ple_w_gate, i), (ple_w_proj, i)]
            nxt_w = None
            if i + 1 < depth:
                stack, jn = inproj(i + 1)
                if lane_aligned(stack):
                    later.append((stack, jn))
                else:
                    nxt_w = stack[jn].astype(BF16)
            if kind == 0:
                h, cast = _sb_layer(h, norm_mix[i], w_in, sb_q_gain[j], sb_k_gain[j], (sb_w_out, j), later)
            elif kind == 1:
                h, cast = _s5_layer(h, norm_mix[i], w_in, s5_lam_re[j], s5_lam_im[j], s5_log_dt[j],
                                    s5_b_re[j], s5_b_im[j], s5_c_re[j], s5_c_im[j], s5_d[j],
                                    (s5_w_glu, j), (s5_w_out, j), later)
            else:
                h, cast = _mlstm_layer(h, norm_mix[i], w_in, ml_gate_bias[j], ml_head_gain[j],
                                       (ml_w_out, j), later)
            w_gate, w_up, w_down, pw_gate, pw_proj, *nxt = cast
            h = ffn(h, norm_ffn[i], w_gate, w_up, w_down)
            h = ple(h, ple_norm[i], p, pw_gate, pw_proj, i, b)
            w_in = nxt[0] if nxt else nxt_w
        outs.append(h)
    return jnp.stack(outs)
```

```python
import functools
import math

import jax
import jax.numpy as jnp
from jax import lax
from jax.experimental import pallas as pl
from jax.experimental.pallas import tpu as pltpu

F32 = jnp.float32
BF16 = jnp.bfloat16

EPS = 1e-6
LANES = 128
SUBLANES = 8
VMEM_LIMIT = 56 * 1024 * 1024
COL_CHUNK = 512

SB_HEAD_DIM = 128
ML_HEADS = 8
EXP2_ZERO_ABOVE = 160.0


def _cparams(*sem):
    return pltpu.CompilerParams(dimension_semantics=sem, vmem_limit_bytes=VMEM_LIMIT)


def _const_spec(shape, layer=None):
    nd = len(shape)
    if layer is None:
        return pl.BlockSpec(shape, lambda *_: (0,) * nd, pipeline_mode=pl.Buffered(1))
    return pl.BlockSpec((None,) + tuple(shape), lambda *_: (layer,) + (0,) * nd,
                        pipeline_mode=pl.Buffered(1))


BF16_SUBLANES = 16


def _side_cast_plan(casts, n_steps, step_of):
    in_specs, out_specs, out_shapes = [], [], []
    for w, layer in casts:
        _, r, c = w.shape
        n_slabs = n_steps
        while r % (n_slabs * BF16_SUBLANES):
            n_slabs //= 2
        rows, per = r // n_slabs, n_steps // n_slabs
        in_specs.append(pl.BlockSpec((None, rows, c),
                                     lambda *g, per=per, layer=layer: (layer, step_of(*g) // per, 0)))
        out_specs.append(pl.BlockSpec((rows, c), lambda *g, per=per: (step_of(*g) // per, 0)))
        out_shapes.append(jax.ShapeDtypeStruct((r, c), BF16))
    return in_specs, out_specs, out_shapes


def _side_cast(in_refs, out_refs):
    for wi, wo in zip(in_refs, out_refs):
        wo[...] = wi[...].astype(wo.dtype)


def _rms_to(dst_ref, x_ref, g_ref, copy_ref=None, chunk=256):
    rows = x_ref.shape[0]
    chunk = min(chunk, rows)

    def body(c, _):
        r = pl.multiple_of(c * chunk, chunk)
        x = x_ref[pl.ds(r, chunk), :]
        ms = jnp.mean(x * x, axis=-1, keepdims=True)
        dst_ref[pl.ds(r, chunk), :] = (x * lax.rsqrt(ms + EPS) * g_ref[...]).astype(dst_ref.dtype)
        if copy_ref is not None:
            copy_ref[pl.ds(r, chunk), :] = x
        return 0

    lax.fori_loop(0, rows // chunk, body, 0)


def _col_chunks(n):
    c = min(COL_CHUNK, n)
    assert n % c == 0
    return [slice(i * c, (i + 1) * c) for i in range(n // c)]


def _log_sigmoid(z):
    return jnp.minimum(z, 0.0) - jnp.log1p(jnp.exp(-jnp.abs(z)))


LOG2_E = 1.4426950408889634


def _softplus_log2(zz):
    return jnp.maximum(zz, 0.0) + jnp.log2(1.0 + jnp.exp2(-jnp.abs(zz)))


def _sigmoid(z):
    return 1.0 / (1.0 + jnp.exp(-z))


def _norm_matmul_kernel(x_ref, g_ref, w_ref, o_ref, xn_ref):
    @pl.when(pl.program_id(1) == 0)
    def _():
        _rms_to(xn_ref, x_ref, g_ref)

    xn = xn_ref[...]
    for sl in _col_chunks(o_ref.shape[1]):
        o_ref[:, sl] = jnp.dot(xn, w_ref[:, sl], preferred_element_type=F32).astype(o_ref.dtype)


def norm_matmul(x, g, w, out_dtype, *, tm, tn, name):
    m, k = x.shape
    n = w.shape[1]
    assert m % tm == 0 and n % tn == 0
    return pl.pallas_call(
        _norm_matmul_kernel,
        out_shape=jax.ShapeDtypeStruct((m, n), out_dtype),
        grid=(m // tm, n // tn),
        in_specs=[
            pl.BlockSpec((tm, k), lambda i, j: (i, 0)),
            pl.BlockSpec((1, k), lambda i, j: (0, 0)),
            pl.BlockSpec((k, tn), lambda i, j: (0, j)),
        ],
        out_specs=pl.BlockSpec((tm, tn), lambda i, j: (i, j)),
        scratch_shapes=[pltpu.VMEM((tm, k), BF16)],
        compiler_params=_cparams("parallel", "arbitrary"),
        name=name,
    )(x, g.reshape(1, k), w)


def _sb_inproj_kernel(x_ref, g_ref, w_ref, hg_ref, o_ref, xn_ref):
    _rms_to(xn_ref, x_ref, g_ref)
    xn = xn_ref[...]
    d = o_ref.shape[1] // 3
    for sl in _col_chunks(3 * d):
        y = jnp.dot(xn, w_ref[:, sl], preferred_element_type=F32)
        region = sl.start // d
        if region == 2:
            o_ref[:, sl] = y.astype(o_ref.dtype)
            continue
        scale = SB_HEAD_DIM ** -0.5 if region == 0 else 1.0
        for h in range((sl.stop - sl.start) // SB_HEAD_DIM):
            blk = y[:, h * SB_HEAD_DIM:(h + 1) * SB_HEAD_DIM]
            ms = jnp.mean(blk * blk, axis=-1, keepdims=True)
            nb = blk * lax.rsqrt(ms + EPS) * hg_ref[region]
            cols = slice(sl.start + h * SB_HEAD_DIM, sl.start + (h + 1) * SB_HEAD_DIM)
            o_ref[:, cols] = (nb * scale).astype(o_ref.dtype)


def sb_inproj(x, g, w, q_gain, k_gain, *, tm=512):
    m, k = x.shape
    n = w.shape[1]
    hg = jnp.stack([q_gain, k_gain]).reshape(2, 1, SB_HEAD_DIM)
    return pl.pallas_call(
        _sb_inproj_kernel,
        out_shape=jax.ShapeDtypeStruct((m, n), BF16),
        grid=(m // tm,),
        in_specs=[
            pl.BlockSpec((tm, k), lambda i: (i, 0)),
            _const_spec((1, k)),
            _const_spec((k, n)),
            _const_spec((2, 1, SB_HEAD_DIM)),
        ],
        out_specs=pl.BlockSpec((tm, n), lambda i: (i, 0)),
        scratch_shapes=[pltpu.VMEM((tm, k), BF16)],
        compiler_params=_cparams("parallel"),
        name="sb_inproj",
    )(x, g.reshape(1, k), w, hg)


def _sb_attn_kernel(q_ref, k_ref, v_ref, suf_ref, *rest, blk, nsub, n_cast):
    cast_in, o_ref, cast_out = rest[:n_cast], rest[n_cast], rest[n_cast + 1:]
    _side_cast(cast_in, cast_out)
    qi = pl.program_id(1)
    suffix = suf_ref[...]

    subs = range(nsub)

    def suffix_sums(lgs, mat):
        his = [lg.astype(BF16) for lg in lgs]
        los = [(lg - hi.astype(F32)).astype(BF16) for lg, hi in zip(lgs, his)]
        return [jnp.dot(hi, mat, preferred_element_type=F32)
                + jnp.dot(lo, mat, preferred_element_type=F32) for hi, lo in zip(his, los)]

    def scores(q, kb):
        r = pl.multiple_of(kb * blk, blk)
        k = k_ref[pl.ds(r, blk), :]
        z = lax.dot_general(q, k, (((1,), (1,)), ((), ())), preferred_element_type=F32)
        return z * LOG2_E, v_ref[pl.ds(r, blk), :]

    def earlier_blocks(kbs, carries):
        zv = [scores(qs[j], jnp.maximum(kbs[j], 0)) for j in subs]
        cs = suffix_sums([_softplus_log2(zz) for zz, _ in zv], suffix)
        a = [jnp.where(kbs[j] >= 0, jnp.exp2(zv[j][0] - cs[j] - carries[j]), 0.0) for j in subs]
        new_c = [carries[j] + jnp.where(kbs[j] >= 0, cs[j][:, 0:1], 0.0) for j in subs]
        pv = [jnp.dot(a[j].astype(BF16), zv[j][1], preferred_element_type=F32) for j in subs]
        return new_c, pv

    qs = [q_ref[j * blk:(j + 1) * blk, :] for j in subs]
    gbs = [qi * nsub + j for j in subs]
    has_prev = [gb >= 1 for gb in gbs]
    strict = (lax.broadcasted_iota(jnp.int32, (blk, blk), 1)
              < lax.broadcasted_iota(jnp.int32, (blk, blk), 0))
    zv_o = [scores(qs[j], gbs[j]) for j in subs]
    zv_p = [scores(qs[j], jnp.maximum(gbs[j] - 1, 0)) for j in subs]
    sp_o = [jnp.where(strict, _softplus_log2(zz), 0.0) for zz, _ in zv_o]
    cs_o = suffix_sums(sp_o, suffix)
    sp_p = [_softplus_log2(zz) for zz, _ in zv_p]
    cs_p = suffix_sums(sp_p, suffix)
    a_o = [jnp.where(strict, jnp.exp2(zv_o[j][0] - cs_o[j]), 0.0) for j in subs]
    acc_o = [jnp.dot(a_o[j].astype(BF16), zv_o[j][1], preferred_element_type=F32) for j in subs]
    c_o = [c[:, 0:1] for c in cs_o]
    a_p = [jnp.where(has_prev[j], jnp.exp2(zv_p[j][0] - cs_p[j] - c_o[j]), 0.0) for j in subs]
    acc_p = [jnp.dot(a_p[j].astype(BF16), zv_p[j][1], preferred_element_type=F32) for j in subs]
    carries = [c_o[j] + jnp.where(has_prev[j], cs_p[j][:, 0:1], 0.0) for j in subs]
    accs = [acc_o[j] + acc_p[j] for j in subs]

    n_more = qi * nsub + nsub - 2

    def cond(state):
        i, carries, _ = state
        live = carries[0]
        for c in carries[1:]:
            live = jnp.minimum(live, c)
        return jnp.logical_and(i < n_more, jnp.min(live) < EXP2_ZERO_ABOVE)

    def body(state):
        i, carries, accs = state
        new_c, pv = earlier_blocks([gbs[j] - 2 - i for j in subs], carries)
        return i + 1, tuple(new_c), tuple(accs[j] + pv[j] for j in subs)

    _, _, accs = lax.while_loop(cond, body, (jnp.int32(0), tuple(carries), tuple(accs)))
    for j in subs:
        o_ref[j * blk:(j + 1) * blk, :] = accs[j].astype(o_ref.dtype)


def sb_attention(qkv, casts=(), *, blk=256, nsub=4):
    seq, n3 = qkv.shape
    d = n3 // 3
    heads = d // SB_HEAD_DIM
    tq = blk * nsub
    assert seq % tq == 0
    nq = seq // tq
    idx = jnp.arange(blk)
    suffix = (idx[:, None] >= idx[None, :]).astype(BF16)
    c_in, c_out, c_shapes = _side_cast_plan(casts, heads * nq, lambda h, i: h * nq + i)
    kern = functools.partial(_sb_attn_kernel, blk=blk, nsub=nsub, n_cast=len(casts))
    return pl.pallas_call(
        kern,
        out_shape=[jax.ShapeDtypeStruct((seq, d), BF16)] + c_shapes,
        grid=(heads, nq),
        in_specs=[
            pl.BlockSpec((tq, SB_HEAD_DIM), lambda h, i: (i, h)),
            pl.BlockSpec((seq, SB_HEAD_DIM), lambda h, i: (0, heads + h)),
            pl.BlockSpec((seq, SB_HEAD_DIM), lambda h, i: (0, 2 * heads + h)),
            _const_spec((blk, blk)),
        ] + c_in,
        out_specs=[pl.BlockSpec((tq, SB_HEAD_DIM), lambda h, i: (i, h))] + c_out,
        compiler_params=_cparams("arbitrary", "arbitrary"),
        name="sb_attention",
    )(qkv, qkv, qkv, suffix, *[w for w, _ in casts])


def _resid_matmul_kernel(x_ref, a_ref, w_ref, o_ref):
    a = a_ref[...]
    for sl in _col_chunks(o_ref.shape[1]):
        o_ref[:, sl] = x_ref[:, sl] + jnp.dot(a, w_ref[:, sl], preferred_element_type=F32)


def resid_matmul(x, a, w, *, tm=512, name="resid_matmul"):
    m, n = x.shape
    k = a.shape[1]
    return pl.pallas_call(
        _resid_matmul_kernel,
        out_shape=jax.ShapeDtypeStruct((m, n), F32),
        grid=(m // tm,),
        in_specs=[
            pl.BlockSpec((tm, n), lambda i: (i, 0)),
            pl.BlockSpec((tm, k), lambda i: (i, 0)),
            _const_spec((k, n)),
        ],
        out_specs=pl.BlockSpec((tm, n), lambda i: (i, 0)),
        compiler_params=_cparams("parallel"),
        name=name,
    )(x, a, w)


def _ffn_kernel(x_ref, g_ref, wg_ref, wu_ref, wd_ref, o_ref, xn_ref):
    @pl.when(pl.program_id(1) == 0)
    def _():
        _rms_to(xn_ref, x_ref, g_ref, copy_ref=o_ref)

    xn = xn_ref[...]
    gate = jnp.dot(xn, wg_ref[...], preferred_element_type=F32)
    up = jnp.dot(xn, wu_ref[...], preferred_element_type=F32)
    hid = (gate * _sigmoid(gate) * up).astype(BF16)
    o_ref[...] += jnp.dot(hid, wd_ref[...], preferred_element_type=F32)


def ffn(x, g, w_gate, w_up, w_down, *, tm=1024, tf=512):
    m, d = x.shape
    dff = w_gate.shape[1]
    assert m % tm == 0 and dff % tf == 0
    return pl.pallas_call(
        _ffn_kernel,
        out_shape=jax.ShapeDtypeStruct((m, d), F32),
        grid=(m // tm, dff // tf),
        in_specs=[
            pl.BlockSpec((tm, d), lambda i, f: (i, 0)),
            pl.BlockSpec((1, d), lambda i, f: (0, 0)),
            pl.BlockSpec((d, tf), lambda i, f: (0, f)),
            pl.BlockSpec((d, tf), lambda i, f: (0, f)),
            pl.BlockSpec((tf, d), lambda i, f: (f, 0)),
        ],
        out_specs=pl.BlockSpec((tm, d), lambda i, f: (i, 0)),
        scratch_shapes=[pltpu.VMEM((tm, d), BF16)],
        compiler_params=_cparams("parallel", "arbitrary"),
        name="ffn",
    )(x, g.reshape(1, d), w_gate, w_up, w_down)


def _ple_kernel(x_ref, g_ref, p_ref, wg_ref, wp_ref, o_ref, xn_ref):
    _rms_to(xn_ref, x_ref, g_ref)
    xn = xn_ref[...]
    pb = p_ref[...].astype(BF16)
    for sl in _col_chunks(o_ref.shape[1]):
        gate = _sigmoid(jnp.dot(xn, wg_ref[:, sl], preferred_element_type=F32))
        proj = jnp.dot(pb, wp_ref[:, sl], preferred_element_type=F32)
        o_ref[:, sl] = x_ref[:, sl] + gate * proj


def ple(x, g, p, w_gate, w_proj, layer, batch, *, tm=512):
    m, d = x.shape
    pd = p.shape[3]
    return pl.pallas_call(
        _ple_kernel,
        out_shape=jax.ShapeDtypeStruct((m, d), F32),
        grid=(m // tm,),
        in_specs=[
            pl.BlockSpec((tm, d), lambda i: (i, 0)),
            _const_spec((1, d)),
            pl.BlockSpec((None, None, tm, pd), lambda i: (layer, batch, i, 0)),
            _const_spec((d, d)),
            _const_spec((pd, d)),
        ],
        out_specs=pl.BlockSpec((tm, d), lambda i: (i, 0)),
        scratch_shapes=[pltpu.VMEM((tm, d), BF16)],
        compiler_params=_cparams("parallel"),
        name="ple",
    )(x, g.reshape(1, d), p, w_gate, w_proj)


def _s5_disc(lr, li, dt):
    mag = jnp.exp(lr * dt)
    ar = mag * jnp.cos(li * dt)
    ai = mag * jnp.sin(li * dt)
    den = lr * lr + li * li
    cr = ((ar - 1.0) * lr + ai * li) / den
    ci = (ai * lr - (ar - 1.0) * li) / den
    return ar, ai, cr, ci


def _s5_params_kernel(lr_ref, li_ref, ldt_ref, lrx_ref, lix_ref, ldtx_ref, bre_ref, bim_ref,
                      ar_ref, ai_ref, bbr_ref, bbi_ref):
    ar, ai, _, _ = _s5_disc(lr_ref[...], li_ref[...], jnp.exp(ldt_ref[...]))
    ar_ref[...] = ar
    ai_ref[...] = ai
    _, _, cr, ci = _s5_disc(lrx_ref[...], lix_ref[...], jnp.exp(ldtx_ref[...]))
    br = bre_ref[...]
    bi = bim_ref[...]
    bbr_ref[...] = cr * br - ci * bi
    bbi_ref[...] = cr * bi + ci * br


def s5_params(lam_re, lam_im, log_dt, b_re, b_im):
    g, p = lam_re.shape
    gs = b_re.shape[-1]
    ldt = jnp.broadcast_to(log_dt[:, None], (g, p))
    rep = lambda t: jnp.repeat(t, gs, axis=1)
    shp = jax.ShapeDtypeStruct
    return pl.pallas_call(
        _s5_params_kernel,
        out_shape=(shp((g, p), F32), shp((g, p), F32), shp((g, p * gs), F32), shp((g, p * gs), F32)),
        name="s5_params",
    )(lam_re, lam_im, ldt, rep(lam_re), rep(lam_im), rep(ldt),
      b_re.reshape(g, p * gs), b_im.reshape(g, p * gs))


S5_NBLK = 8
S5_T = 128
S5_PITCH = S5_T + 4


def _gelu_tanh(y):
    return 0.5 * y * (1.0 + jnp.tanh(math.sqrt(2.0 / math.pi) * (y + 0.044715 * (y * y * y))))


def _s5_scan_kernel(u_ref, bre_ref, bim_ref, cre_ref, cim_ref, ar_ref, ai_ref, d_ref, *rest, n_cast):
    cast_in, o_ref, cast_out = rest[:n_cast], rest[n_cast], rest[n_cast + 1:2 * n_cast + 1]
    xr_ref, xi_ref, sr_ref, si_ref = rest[2 * n_cast + 1:]
    br_ref, bi_ref = xr_ref, xi_ref
    _side_cast(cast_in, cast_out)
    nslab = xr_ref.shape[0]
    ucols = u_ref.shape[1] // S5_NBLK

    @pl.when(pl.program_id(0) == 0)
    def _():
        sr_ref[...] = jnp.zeros_like(sr_ref)
        si_ref[...] = jnp.zeros_like(si_ref)

    for kb in range(S5_NBLK):
        ukb = u_ref[:, kb * ucols:(kb + 1) * ucols].astype(BF16)
        bur = jnp.dot(ukb, bre_ref[kb], preferred_element_type=F32)
        bui = jnp.dot(ukb, bim_ref[kb], preferred_element_type=F32)
        for cs in range(nslab):
            br_ref[cs, kb * S5_PITCH:kb * S5_PITCH + S5_T, :] = bur[:, cs * LANES:(cs + 1) * LANES]
            bi_ref[cs, kb * S5_PITCH:kb * S5_PITCH + S5_T, :] = bui[:, cs * LANES:(cs + 1) * LANES]

    def step(t, carry):
        new = []
        for cs in range(nslab):
            sr, si = carry[cs]
            ar = ar_ref[:, cs * LANES:(cs + 1) * LANES]
            ai = ai_ref[:, cs * LANES:(cs + 1) * LANES]
            rows = pl.ds(t, S5_NBLK, stride=S5_PITCH)
            nr = ar * sr - ai * si + br_ref[cs, rows, :]
            ni = ar * si + ai * sr + bi_ref[cs, rows, :]
            xr_ref[cs, rows, :] = nr
            xi_ref[cs, rows, :] = ni
            new.append((nr, ni))
        return tuple(new)

    init = tuple((sr_ref[:, cs * LANES:(cs + 1) * LANES], si_ref[:, cs * LANES:(cs + 1) * LANES])
                 for cs in range(nslab))
    final = lax.fori_loop(0, S5_T, step, init, unroll=2)
    for cs in range(nslab):
        sr_ref[:, cs * LANES:(cs + 1) * LANES] = final[cs][0]
        si_ref[:, cs * LANES:(cs + 1) * LANES] = final[cs][1]

    for kb in range(S5_NBLK):
        rows = slice(kb * S5_PITCH, kb * S5_PITCH + S5_T)
        xr = jnp.concatenate([xr_ref[cs, rows, :] for cs in range(nslab)], axis=1).astype(BF16)
        xi = jnp.concatenate([xi_ref[cs, rows, :] for cs in range(nslab)], axis=1).astype(BF16)
        y = (jnp.dot(xr, cre_ref[kb], preferred_element_type=F32)
             - jnp.dot(xi, cim_ref[kb], preferred_element_type=F32))
        sl = slice(kb * ucols, (kb + 1) * ucols)
        y = y + d_ref[:, sl] * u_ref[:, sl]
        o_ref[:, sl] = _gelu_tanh(y)


def s5_scan(u, bblk, cblk, a_re, a_im, d_skip, casts=()):
    seq, d = u.shape
    half = lambda shape, which: pl.BlockSpec(shape, lambda c: (which, 0, 0), pipeline_mode=pl.Buffered(1))
    nstate = a_re.size
    blk_states = nstate // S5_NBLK
    nslab = blk_states // LANES
    ucols = d // S5_NBLK
    nc = seq // S5_T
    c_in, c_out, c_shapes = _side_cast_plan(casts, nc, lambda c: c)
    return pl.pallas_call(
        functools.partial(_s5_scan_kernel, n_cast=len(casts)),
        out_shape=[jax.ShapeDtypeStruct((seq, d), F32)] + c_shapes,
        grid=(nc,),
        in_specs=[
            pl.BlockSpec((S5_T, d), lambda c: (c, 0)),
            half((S5_NBLK, ucols, blk_states), 0),
            half((S5_NBLK, ucols, blk_states), 1),
            half((S5_NBLK, blk_states, ucols), 0),
            half((S5_NBLK, blk_states, ucols), 1),
            _const_spec((S5_NBLK, blk_states)),
            _const_spec((S5_NBLK, blk_states)),
            _const_spec((1, d)),
        ] + c_in,
        out_specs=[pl.BlockSpec((S5_T, d), lambda c: (c, 0))] + c_out,
        scratch_shapes=[
            pltpu.VMEM((nslab, S5_NBLK * S5_PITCH, LANES), F32),
            pltpu.VMEM((nslab, S5_NBLK * S5_PITCH, LANES), F32),
            pltpu.VMEM((S5_NBLK, blk_states), F32),
            pltpu.VMEM((S5_NBLK, blk_states), F32),
        ],
        compiler_params=_cparams("arbitrary"),
        name="s5_scan",
    )(u, bblk, bblk, cblk, cblk,
      a_re.reshape(S5_NBLK, blk_states), a_im.reshape(S5_NBLK, blk_states), d_skip.reshape(1, d),
      *[w for w, _ in casts])


def _glu_out_kernel(x_ref, g_ref, wglu_ref, wout_ref, o_ref, h_ref):
    gb = g_ref[...].astype(BF16)
    chunks = _col_chunks(o_ref.shape[1])
    for sl in chunks:
        gate = _sigmoid(jnp.dot(gb, wglu_ref[:, sl], preferred_element_type=F32))
        h_ref[:, sl] = (g_ref[:, sl] * gate).astype(BF16)
    h = h_ref[...]
    for sl in chunks:
        o_ref[:, sl] = x_ref[:, sl] + jnp.dot(h, wout_ref[:, sl], preferred_element_type=F32)


def glu_out(x, g, w_glu, w_out, *, tm=512):
    m, d = x.shape
    return pl.pallas_call(
        _glu_out_kernel,
        out_shape=jax.ShapeDtypeStruct((m, d), F32),
        grid=(m // tm,),
        in_specs=[
            pl.BlockSpec((tm, d), lambda i: (i, 0)),
            pl.BlockSpec((tm, d), lambda i: (i, 0)),
            _const_spec((d, d)),
            _const_spec((d, d)),
        ],
        out_specs=pl.BlockSpec((tm, d), lambda i: (i, 0)),
        scratch_shapes=[pltpu.VMEM((tm, d), BF16)],
        compiler_params=_cparams("parallel"),
        name="s5_glu_out",
    )(x, g, w_glu, w_out)


def _ml_inproj_kernel(x_ref, g_ref, w_ref, wg_ref, qkv_ref, o_ref, gates_ref, xn_ref):
    _rms_to(xn_ref, x_ref, g_ref)
    xn = xn_ref[...]
    n_qkv = qkv_ref.shape[1]
    gates_ref[...] = jnp.dot(xn, wg_ref[...], preferred_element_type=F32)
    for sl in _col_chunks(n_qkv):
        qkv_ref[:, sl] = jnp.dot(xn, w_ref[:, sl], preferred_element_type=F32).astype(qkv_ref.dtype)
    for sl in _col_chunks(o_ref.shape[1]):
        wcols = slice(n_qkv + sl.start, n_qkv + sl.stop)
        o_ref[:, sl] = jnp.dot(xn, w_ref[:, wcols], preferred_element_type=F32)


def ml_inproj(x, g, w, w_gates, n_qkv, n_o, *, tm=512):
    m, k = x.shape
    shp = jax.ShapeDtypeStruct
    return pl.pallas_call(
        _ml_inproj_kernel,
        out_shape=(shp((m, n_qkv), BF16), shp((m, n_o), F32), shp((m, LANES), F32)),
        grid=(m // tm,),
        in_specs=[
            pl.BlockSpec((tm, k), lambda i: (i, 0)),
            _const_spec((1, k)),
            _const_spec((k, w.shape[1])),
            _const_spec((k, LANES)),
        ],
        out_specs=(
            pl.BlockSpec((tm, n_qkv), lambda i: (i, 0)),
            pl.BlockSpec((tm, n_o), lambda i: (i, 0)),
            pl.BlockSpec((tm, LANES), lambda i: (i, 0)),
        ),
        scratch_shapes=[pltpu.VMEM((tm, k), BF16)],
        compiler_params=_cparams("parallel"),
        name="ml_inproj",
    )(x, g.reshape(1, k), w, w_gates)


ML_T = 128


def _mlstm_kernel(q_ref, k_ref, v_ref, o_ref, gc_ref, gr_ref, bc_ref, br_ref, hg_ref, *rest, n_cast):
    cast_in, out_ref, cast_out = rest[:n_cast], rest[n_cast], rest[n_cast + 1:2 * n_cast + 1]
    c_ref, m_ref = rest[2 * n_cast + 1:]
    _side_cast(cast_in, cast_out)
    t = ML_T
    nh = ML_HEADS
    dqk = q_ref.shape[1] // nh
    dv = v_ref.shape[1] // nh
    kscale = dqk ** -0.5

    @pl.when(pl.program_id(0) == 0)
    def _():
        c_ref[...] = jnp.zeros_like(c_ref)
        m_ref[...] = jnp.zeros_like(m_ref)

    row = lax.broadcasted_iota(jnp.int32, (t, t), 0)
    col = lax.broadcasted_iota(jnp.int32, (t, t), 1)
    causal = col <= row
    lower = causal.astype(F32)
    upper = (row <= col).astype(F32)
    eye = (lax.broadcasted_iota(jnp.int32, (dqk, dqk), 0)
           == lax.broadcasted_iota(jnp.int32, (dqk, dqk), 1)).astype(BF16)
    ones_col = (lax.broadcasted_iota(jnp.int32, (t, LANES), 1) == 0).astype(BF16)

    gcol = gc_ref[...] + bc_ref[...]
    lf_col = _log_sigmoid(gcol)
    bcum_col = jnp.dot(lower, lf_col, preferred_element_type=F32, precision=lax.Precision.HIGHEST)
    grow = gr_ref[...] + br_ref[...]
    lf_row = _log_sigmoid(grow)
    bcum_row = jnp.dot(lf_row, upper, preferred_element_type=F32, precision=lax.Precision.HIGHEST)

    m_all = m_ref[...]
    heads = range(nh)
    nt = (((1,), (1,)), ((), ()))
    qs = [q_ref[:, h * dqk:(h + 1) * dqk] for h in heads]
    ks = [k_ref[:, h * dqk:(h + 1) * dqk] for h in heads]
    vs = [v_ref[:, h * dv:(h + 1) * dv] for h in heads]
    cs = [c_ref[h] for h in heads]
    bcs = [bcum_col[:, nh + h:nh + h + 1] for h in heads]
    brs = [bcum_row[nh + h:nh + h + 1, :] for h in heads]
    irs = [grow[h:h + 1, :] for h in heads]
    ms_ = [m_all[h:h + 1, 0:1] for h in heads]

    qk = [lax.dot_general(qs[h], ks[h], nt, preferred_element_type=F32) for h in heads]
    qc = [jnp.dot(qs[h], cs[h].astype(BF16), preferred_element_type=F32) for h in heads]
    kt = [lax.dot_general(eye, ks[h], nt, preferred_element_type=F32) for h in heads]

    dmat = [jnp.where(causal, bcs[h] - brs[h] + irs[h], -jnp.inf) for h in heads]
    inter = [bcs[h] + ms_[h] for h in heads]
    m_row = [jnp.maximum(jnp.max(dmat[h], axis=1, keepdims=True), inter[h]) for h in heads]
    s = [qk[h] * (jnp.exp(dmat[h] - m_row[h]) * kscale) for h in heads]
    w_inter = [jnp.exp(inter[h] - m_row[h]) for h in heads]
    sv = [jnp.dot(s[h].astype(BF16), vs[h], preferred_element_type=F32) for h in heads]

    b_last = [brs[h][:, t - 1:t] for h in heads]
    dec = [b_last[h] - brs[h] + irs[h] for h in heads]
    m_new = [jnp.maximum(b_last[h] + ms_[h], jnp.max(dec[h], axis=1, keepdims=True)) for h in heads]
    wk = [jnp.exp(dec[h] - m_new[h]) * kscale for h in heads]
    carry_scale = [jnp.exp(b_last[h] + ms_[h] - m_new[h]) for h in heads]
    for h in heads:
        v_ext = jnp.concatenate([vs[h], ones_col], axis=1)
        c_ref[h] = carry_scale[h] * cs[h] + jnp.dot((kt[h] * wk[h]).astype(BF16), v_ext,
                                                    preferred_element_type=F32)
    m_ref[...] = jnp.concatenate([jnp.broadcast_to(m_new[h], (1, m_ref.shape[1])) for h in heads],
                                 axis=0)

    num = [sv[h] + w_inter[h] * qc[h][:, :dv] for h in heads]
    den = [jnp.sum(s[h], axis=1, keepdims=True) + w_inter[h] * qc[h][:, dv:dv + 1] for h in heads]
    h_t = [num[h] / jnp.maximum(jnp.abs(den[h]), jnp.exp(-m_row[h])) for h in heads]
    msq = [jnp.mean(h_t[h] * h_t[h], axis=1, keepdims=True) for h in heads]
    for h in heads:
        hn = h_t[h] * lax.rsqrt(msq[h] + EPS) * hg_ref[:, h * dv:(h + 1) * dv]
        og = _sigmoid(o_ref[:, h * dv:(h + 1) * dv])
        out_ref[:, h * dv:(h + 1) * dv] = (og * hn).astype(out_ref.dtype)


def mlstm_scan(qkv, o, g_col, g_row, bias, head_gain, casts=()):
    seq = qkv.shape[0]
    d = o.shape[1]
    nqk = (qkv.shape[1] - d) // 2
    nh = ML_HEADS
    assert d == 2 * nqk
    bias_col = jnp.zeros((1, LANES), F32).at[0, :2 * nh].set(bias)
    bias_row = bias.reshape(2 * nh, 1)
    nc = seq // ML_T
    c_in, c_out, c_shapes = _side_cast_plan(casts, nc, lambda c: c)
    return pl.pallas_call(
        functools.partial(_mlstm_kernel, n_cast=len(casts)),
        out_shape=[jax.ShapeDtypeStruct((seq, d), BF16)] + c_shapes,
        grid=(nc,),
        in_specs=[
            pl.BlockSpec((ML_T, nqk), lambda c: (c, 0)),
            pl.BlockSpec((ML_T, nqk), lambda c: (c, 1)),
            pl.BlockSpec((ML_T, d), lambda c: (c, 1)),
            pl.BlockSpec((ML_T, d), lambda c: (c, 0)),
            pl.BlockSpec((ML_T, LANES), lambda c: (c, 0)),
            pl.BlockSpec((2 * nh, ML_T), lambda c: (0, c)),
            _const_spec((1, LANES)),
            _const_spec((2 * nh, 1)),
            _const_spec((1, d)),
        ] + c_in,
        out_specs=[pl.BlockSpec((ML_T, d), lambda c: (c, 0))] + c_out,
        scratch_shapes=[
            pltpu.VMEM((nh, nqk // nh, d // nh + LANES), F32),
            pltpu.VMEM((nh, LANES), F32),
        ],
        compiler_params=_cparams("arbitrary"),
        name="mlstm_scan",
    )(qkv, qkv, qkv, o, g_col, g_row, bias_col, bias_row, head_gain.reshape(1, d),
      *[w for w, _ in casts])


def _run_casts(core, casts, ready):
    out, *cast = core([(stack, ix) for _, stack, ix in casts])
    ready.update({key: w for (key, _, _), w in zip(casts, cast)})
    return out


def _sb_layer(x, g, w_in, q_gain, k_gain, out_key, casts, ready):
    qkv = sb_inproj(x, g, w_in, q_gain, k_gain)
    o = _run_casts(functools.partial(sb_attention, qkv), casts, ready)
    return resid_matmul(x, o, ready[out_key], name="sb_out")


def _block_diag(t):
    nb, gb, r, c = t.shape
    eye = jnp.eye(gb, dtype=t.dtype)
    return (t[:, :, :, None, :] * eye[None, :, None, :, None]).reshape(nb, gb * r, gb * c)


def _s5_layer(x, g, w_in, lam_re, lam_im, log_dt, b_re, b_im, c_re, c_im, d_skip, glu_key, out_key,
              casts, ready):
    ngrp, nst = lam_re.shape
    gs = b_re.shape[-1]
    gb = ngrp // S5_NBLK
    d = x.shape[1]
    u = norm_matmul(x, g, w_in, F32, tm=512, tn=d, name="s5_in")
    a_re, a_im, bb_re, bb_im = s5_params(lam_re, lam_im, log_dt, b_re, b_im)
    stacked = lambda re, im, r, c: jnp.stack([re, im]).astype(BF16).reshape(2 * S5_NBLK, gb, r, c)
    bblk = _block_diag(stacked(bb_re, bb_im, nst, gs).transpose(0, 1, 3, 2))
    cblk = _block_diag(stacked(c_re, c_im, gs, nst).transpose(0, 1, 3, 2))
    core = functools.partial(s5_scan, u, bblk, cblk, a_re, a_im, d_skip)
    y = _run_casts(core, casts, ready)
    return glu_out(x, y, ready[glu_key], ready[out_key])


def _mlstm_layer(x, g, w_in, gate_bias, head_gain, out_key, casts, ready):
    d = x.shape[1]
    nh = ML_HEADS
    nqk = d // 2
    n_qkv, n_o = 2 * nqk + d, d
    w_g = jnp.pad(w_in[:, n_qkv + n_o:], ((0, 0), (0, LANES - 2 * nh)))
    qkv, o, g_col = ml_inproj(x, g, w_in, w_g, n_qkv, n_o)
    g_row = g_col[:, :2 * nh].T
    core = functools.partial(mlstm_scan, qkv, o, g_col, g_row, gate_bias, head_gain)
    hs = _run_casts(core, casts, ready)
    return resid_matmul(x, hs, ready[out_key], name="ml_out")


def kernel(x, p, norm_mix, norm_ffn, sb_w_in, sb_q_gain, sb_k_gain, sb_w_out, s5_w_in, s5_lam_re, s5_lam_im, s5_log_dt, s5_b_re, s5_b_im, s5_c_re, s5_c_im, s5_d, s5_w_glu, s5_w_out, ml_w_in, ml_gate_bias, ml_head_gain, ml_w_out, ffn_w_gate, ffn_w_up, ffn_w_down, ple_norm, ple_w_gate, ple_w_proj):
    bsz, seq, d = x.shape
    depth = norm_mix.shape[0]
    mixer_w = {0: [("out", sb_w_out)], 1: [("glu", s5_w_glu), ("out", s5_w_out)], 2: [("out", ml_w_out)]}
    inproj_w = (sb_w_in, s5_w_in, ml_w_in)

    def needs(i):
        kind, j = i % 3, i // 3
        own = [((i, name), stack, j) for name, stack in mixer_w[kind]]
        shared = [((i, "ffn_gate"), ffn_w_gate, i), ((i, "ffn_up"), ffn_w_up, i),
                  ((i, "ffn_down"), ffn_w_down, i), ((i, "ple_gate"), ple_w_gate, i),
                  ((i, "ple_proj"), ple_w_proj, i)]
        nxt = []
        if i + 1 < depth and inproj_w[(i + 1) % 3].shape[-1] % LANES == 0:
            nxt = [((i + 1, "in"), inproj_w[(i + 1) % 3], (i + 1) // 3)]
        return own + shared + nxt

    host = {i: (i - 1 if i % 3 == 2 and i >= 1 else i) for i in range(depth)}
    casts = {i: [] for i in range(depth)}
    for i in range(depth):
        casts[host[i]] += needs(i)

    outs = []
    for b in range(bsz):
        h = x[b]
        ready = {}
        for i in range(depth):
            kind, j = i % 3, i // 3
            if (i, "in") not in ready:
                ready[(i, "in")] = inproj_w[kind][j].astype(BF16)
            w_in = ready[(i, "in")]
            if kind == 0:
                h = _sb_layer(h, norm_mix[i], w_in, sb_q_gain[j], sb_k_gain[j], (i, "out"), casts[i], ready)
            elif kind == 1:
                h = _s5_layer(h, norm_mix[i], w_in, s5_lam_re[j], s5_lam_im[j], s5_log_dt[j],
                              s5_b_re[j], s5_b_im[j], s5_c_re[j], s5_c_im[j], s5_d[j],
                              (i, "glu"), (i, "out"), casts[i], ready)
            else:
                h = _mlstm_layer(h, norm_mix[i], w_in, ml_gate_bias[j], ml_head_gain[j], (i, "out"),
                                 casts[i], ready)
            h = ffn(h, norm_ffn[i], ready[(i, "ffn_gate")], ready[(i, "ffn_up")], ready[(i, "ffn_down")])
            h = ple(h, ple_norm[i], p, ready[(i, "ple_gate")], ready[(i, "ple_proj")], i, b)
        outs.append(h)
    return jnp.stack(outs)
```

```python
import functools
import math

import jax
import jax.numpy as jnp
from jax import lax
from jax.experimental import pallas as pl
from jax.experimental.pallas import tpu as pltpu

F32 = jnp.float32
BF16 = jnp.bfloat16

EPS = 1e-6
LANES = 128
SUBLANES = 8
VMEM_LIMIT = 56 * 1024 * 1024
COL_CHUNK = 512

SB_HEAD_DIM = 128
SB_SUB_GROUP = 4
ML_HEADS = 8
EXP2_ZERO_ABOVE = 160.0


def _cparams(*sem):
    return pltpu.CompilerParams(dimension_semantics=sem, vmem_limit_bytes=VMEM_LIMIT)


def _const_spec(shape, layer=None):
    nd = len(shape)
    if layer is None:
        return pl.BlockSpec(shape, lambda *_: (0,) * nd, pipeline_mode=pl.Buffered(1))
    return pl.BlockSpec((None,) + tuple(shape), lambda *_: (layer,) + (0,) * nd,
                        pipeline_mode=pl.Buffered(1))


BF16_SUBLANES = 16


def _side_cast_plan(casts, n_steps, step_of):
    in_specs, out_specs, out_shapes = [], [], []
    for w, layer in casts:
        _, r, c = w.shape
        n_slabs = n_steps
        while r % (n_slabs * BF16_SUBLANES):
            n_slabs //= 2
        rows, per = r // n_slabs, n_steps // n_slabs
        in_specs.append(pl.BlockSpec((None, rows, c),
                                     lambda *g, per=per, layer=layer: (layer, step_of(*g) // per, 0)))
        out_specs.append(pl.BlockSpec((rows, c), lambda *g, per=per: (step_of(*g) // per, 0)))
        out_shapes.append(jax.ShapeDtypeStruct((r, c), BF16))
    return in_specs, out_specs, out_shapes


def _side_cast(in_refs, out_refs):
    for wi, wo in zip(in_refs, out_refs):
        wo[...] = wi[...].astype(wo.dtype)


def _rms_to(dst_ref, x_ref, g_ref, copy_ref=None, chunk=256):
    rows = x_ref.shape[0]
    chunk = min(chunk, rows)

    def body(c, _):
        r = pl.multiple_of(c * chunk, chunk)
        x = x_ref[pl.ds(r, chunk), :]
        ms = jnp.mean(x * x, axis=-1, keepdims=True)
        dst_ref[pl.ds(r, chunk), :] = (x * lax.rsqrt(ms + EPS) * g_ref[...]).astype(dst_ref.dtype)
        if copy_ref is not None:
            copy_ref[pl.ds(r, chunk), :] = x
        return 0

    lax.fori_loop(0, rows // chunk, body, 0)


def _col_chunks(n):
    c = min(COL_CHUNK, n)
    assert n % c == 0
    return [slice(i * c, (i + 1) * c) for i in range(n // c)]


def _log_sigmoid(z):
    return jnp.minimum(z, 0.0) - jnp.log1p(jnp.exp(-jnp.abs(z)))


LOG2_E = 1.4426950408889634


def _softplus_log2(zz):
    return jnp.maximum(zz, 0.0) + jnp.log2(1.0 + jnp.exp2(-jnp.abs(zz)))


def _sigmoid(z):
    return 1.0 / (1.0 + jnp.exp(-z))


def _norm_matmul_kernel(x_ref, g_ref, w_ref, o_ref, xn_ref):
    @pl.when(pl.program_id(1) == 0)
    def _():
        _rms_to(xn_ref, x_ref, g_ref)

    xn = xn_ref[...]
    for sl in _col_chunks(o_ref.shape[1]):
        o_ref[:, sl] = jnp.dot(xn, w_ref[:, sl], preferred_element_type=F32).astype(o_ref.dtype)


def norm_matmul(x, g, w, out_dtype, *, tm, tn, name):
    m, k = x.shape
    n = w.shape[1]
    assert m % tm == 0 and n % tn == 0
    return pl.pallas_call(
        _norm_matmul_kernel,
        out_shape=jax.ShapeDtypeStruct((m, n), out_dtype),
        grid=(m // tm, n // tn),
        in_specs=[
            pl.BlockSpec((tm, k), lambda i, j: (i, 0)),
            pl.BlockSpec((1, k), lambda i, j: (0, 0)),
            pl.BlockSpec((k, tn), lambda i, j: (0, j)),
        ],
        out_specs=pl.BlockSpec((tm, tn), lambda i, j: (i, j)),
        scratch_shapes=[pltpu.VMEM((tm, k), BF16)],
        compiler_params=_cparams("parallel", "arbitrary"),
        name=name,
    )(x, g.reshape(1, k), w)


def _sb_inproj_kernel(x_ref, g_ref, w_ref, hg_ref, o_ref, xn_ref):
    _rms_to(xn_ref, x_ref, g_ref)
    xn = xn_ref[...]
    d = o_ref.shape[1] // 3
    for sl in _col_chunks(3 * d):
        y = jnp.dot(xn, w_ref[:, sl], preferred_element_type=F32)
        region = sl.start // d
        if region == 2:
            o_ref[:, sl] = y.astype(o_ref.dtype)
            continue
        scale = SB_HEAD_DIM ** -0.5 if region == 0 else 1.0
        for h in range((sl.stop - sl.start) // SB_HEAD_DIM):
            blk = y[:, h * SB_HEAD_DIM:(h + 1) * SB_HEAD_DIM]
            ms = jnp.mean(blk * blk, axis=-1, keepdims=True)
            nb = blk * lax.rsqrt(ms + EPS) * hg_ref[region]
            cols = slice(sl.start + h * SB_HEAD_DIM, sl.start + (h + 1) * SB_HEAD_DIM)
            o_ref[:, cols] = (nb * scale).astype(o_ref.dtype)


def sb_inproj(x, g, w, q_gain, k_gain, *, tm=512):
    m, k = x.shape
    n = w.shape[1]
    hg = jnp.stack([q_gain, k_gain]).reshape(2, 1, SB_HEAD_DIM)
    return pl.pallas_call(
        _sb_inproj_kernel,
        out_shape=jax.ShapeDtypeStruct((m, n), BF16),
        grid=(m // tm,),
        in_specs=[
            pl.BlockSpec((tm, k), lambda i: (i, 0)),
            _const_spec((1, k)),
            _const_spec((k, n)),
            _const_spec((2, 1, SB_HEAD_DIM)),
        ],
        out_specs=pl.BlockSpec((tm, n), lambda i: (i, 0)),
        scratch_shapes=[pltpu.VMEM((tm, k), BF16)],
        compiler_params=_cparams("parallel"),
        name="sb_inproj",
    )(x, g.reshape(1, k), w, hg)


def _sb_attn_kernel(q_ref, k_ref, v_ref, suf_ref, *rest, blk, nsub, n_cast):
    cast_in, o_ref, cast_out = rest[:n_cast], rest[n_cast], rest[n_cast + 1:]
    _side_cast(cast_in, cast_out)
    qi = pl.program_id(1)
    suffix = suf_ref[...]

    subs = range(nsub)

    def suffix_sums(lgs, mat):
        his = [lg.astype(BF16) for lg in lgs]
        los = [(lg - hi.astype(F32)).astype(BF16) for lg, hi in zip(lgs, his)]
        return [jnp.dot(hi, mat, preferred_element_type=F32)
                + jnp.dot(lo, mat, preferred_element_type=F32) for hi, lo in zip(his, los)]

    def scores(q, kb):
        r = pl.multiple_of(kb * blk, blk)
        k = k_ref[pl.ds(r, blk), :]
        z = lax.dot_general(q, k, (((1,), (1,)), ((), ())), preferred_element_type=F32)
        return z * LOG2_E, v_ref[pl.ds(r, blk), :]

    def earlier_blocks(kbs, carries):
        zv = [scores(qs[j], jnp.maximum(kbs[j], 0)) for j in subs]
        cs = suffix_sums([_softplus_log2(zz) for zz, _ in zv], suffix)
        a = [jnp.where(kbs[j] >= 0, jnp.exp2(zv[j][0] - cs[j] - carries[j]), 0.0) for j in subs]
        new_c = [carries[j] + jnp.where(kbs[j] >= 0, cs[j][:, 0:1], 0.0) for j in subs]
        pv = [jnp.dot(a[j].astype(BF16), zv[j][1], preferred_element_type=F32) for j in subs]
        return new_c, pv

    qs = [q_ref[j * blk:(j + 1) * blk, :] for j in subs]
    gbs = [qi * nsub + j for j in subs]
    strict = (lax.broadcasted_iota(jnp.int32, (blk, blk), 1)
              < lax.broadcasted_iota(jnp.int32, (blk, blk), 0))

    def own_and_prev(js):
        n = range(len(js))
        has_prev = [gbs[j] >= 1 for j in js]
        zv_o = [scores(qs[j], gbs[j]) for j in js]
        zv_p = [scores(qs[j], jnp.maximum(gbs[j] - 1, 0)) for j in js]
        sp_o = [jnp.where(strict, _softplus_log2(zz), 0.0) for zz, _ in zv_o]
        cs_o = suffix_sums(sp_o, suffix)
        sp_p = [_softplus_log2(zz) for zz, _ in zv_p]
        cs_p = suffix_sums(sp_p, suffix)
        a_o = [jnp.where(strict, jnp.exp2(zv_o[i][0] - cs_o[i]), 0.0) for i in n]
        acc_o = [jnp.dot(a_o[i].astype(BF16), zv_o[i][1], preferred_element_type=F32) for i in n]
        c_o = [c[:, 0:1] for c in cs_o]
        a_p = [jnp.where(has_prev[i], jnp.exp2(zv_p[i][0] - cs_p[i] - c_o[i]), 0.0) for i in n]
        acc_p = [jnp.dot(a_p[i].astype(BF16), zv_p[i][1], preferred_element_type=F32) for i in n]
        return ([c_o[i] + jnp.where(has_prev[i], cs_p[i][:, 0:1], 0.0) for i in n],
                [acc_o[i] + acc_p[i] for i in n])

    carries, accs = [], []
    for j0 in range(0, nsub, SB_SUB_GROUP):
        c, a = own_and_prev(list(range(j0, j0 + SB_SUB_GROUP)))
        carries += c
        accs += a

    n_more = qi * nsub + nsub - 2

    def cond(state):
        i, carries, _ = state
        live = carries[0]
        for c in carries[1:]:
            live = jnp.minimum(live, c)
        return jnp.logical_and(i < n_more, jnp.min(live) < EXP2_ZERO_ABOVE)

    def body(state):
        i, carries, accs = state
        new_c, pv = earlier_blocks([gbs[j] - 2 - i for j in subs], carries)
        return i + 1, tuple(new_c), tuple(accs[j] + pv[j] for j in subs)

    _, _, accs = lax.while_loop(cond, body, (jnp.int32(0), tuple(carries), tuple(accs)))
    for j in subs:
        o_ref[j * blk:(j + 1) * blk, :] = accs[j].astype(o_ref.dtype)


def sb_attention(qkv, casts=(), *, blk=256, nsub=4):
    seq, n3 = qkv.shape
    d = n3 // 3
    heads = d // SB_HEAD_DIM
    tq = blk * nsub
    assert seq % tq == 0
    nq = seq // tq
    idx = jnp.arange(blk)
    suffix = (idx[:, None] >= idx[None, :]).astype(BF16)
    c_in, c_out, c_shapes = _side_cast_plan(casts, heads * nq, lambda h, i: h * nq + i)
    kern = functools.partial(_sb_attn_kernel, blk=blk, nsub=nsub, n_cast=len(casts))
    return pl.pallas_call(
        kern,
        out_shape=[jax.ShapeDtypeStruct((seq, d), BF16)] + c_shapes,
        grid=(heads, nq),
        in_specs=[
            pl.BlockSpec((tq, SB_HEAD_DIM), lambda h, i: (i, h)),
            pl.BlockSpec((seq, SB_HEAD_DIM), lambda h, i: (0, heads + h)),
            pl.BlockSpec((seq, SB_HEAD_DIM), lambda h, i: (0, 2 * heads + h)),
            _const_spec((blk, blk)),
        ] + c_in,
        out_specs=[pl.BlockSpec((tq, SB_HEAD_DIM), lambda h, i: (i, h))] + c_out,
        compiler_params=_cparams("arbitrary", "arbitrary"),
        name="sb_attention",
    )(qkv, qkv, qkv, suffix, *[w for w, _ in casts])


def _resid_matmul_kernel(x_ref, a_ref, w_ref, o_ref):
    a = a_ref[...]
    for sl in _col_chunks(o_ref.shape[1]):
        o_ref[:, sl] = x_ref[:, sl] + jnp.dot(a, w_ref[:, sl], preferred_element_type=F32)


def resid_matmul(x, a, w, *, tm=512, name="resid_matmul"):
    m, n = x.shape
    k = a.shape[1]
    return pl.pallas_call(
        _resid_matmul_kernel,
        out_shape=jax.ShapeDtypeStruct((m, n), F32),
        grid=(m // tm,),
        in_specs=[
            pl.BlockSpec((tm, n), lambda i: (i, 0)),
            pl.BlockSpec((tm, k), lambda i: (i, 0)),
            _const_spec((k, n)),
        ],
        out_specs=pl.BlockSpec((tm, n), lambda i: (i, 0)),
        compiler_params=_cparams("parallel"),
        name=name,
    )(x, a, w)


def _ffn_kernel(x_ref, g_ref, wg_ref, wu_ref, wd_ref, o_ref, xn_ref):
    @pl.when(pl.program_id(1) == 0)
    def _():
        _rms_to(xn_ref, x_ref, g_ref, copy_ref=o_ref)

    xn = xn_ref[...]
    gate = jnp.dot(xn, wg_ref[...], preferred_element_type=F32)
    up = jnp.dot(xn, wu_ref[...], preferred_element_type=F32)
    hid = (gate * _sigmoid(gate) * up).astype(BF16)
    o_ref[...] += jnp.dot(hid, wd_ref[...], preferred_element_type=F32)


def ffn(x, g, w_gate, w_up, w_down, *, tm=1024, tf=512):
    m, d = x.shape
    dff = w_gate.shape[1]
    assert m % tm == 0 and dff % tf == 0
    return pl.pallas_call(
        _ffn_kernel,
        out_shape=jax.ShapeDtypeStruct((m, d), F32),
        grid=(m // tm, dff // tf),
        in_specs=[
            pl.BlockSpec((tm, d), lambda i, f: (i, 0)),
            pl.BlockSpec((1, d), lambda i, f: (0, 0)),
            pl.BlockSpec((d, tf), lambda i, f: (0, f)),
            pl.BlockSpec((d, tf), lambda i, f: (0, f)),
            pl.BlockSpec((tf, d), lambda i, f: (f, 0)),
        ],
        out_specs=pl.BlockSpec((tm, d), lambda i, f: (i, 0)),
        scratch_shapes=[pltpu.VMEM((tm, d), BF16)],
        compiler_params=_cparams("parallel", "arbitrary"),
        name="ffn",
    )(x, g.reshape(1, d), w_gate, w_up, w_down)


def _ple_kernel(x_ref, g_ref, p_ref, wg_ref, wp_ref, o_ref, xn_ref):
    _rms_to(xn_ref, x_ref, g_ref)
    xn = xn_ref[...]
    pb = p_ref[...].astype(BF16)
    for sl in _col_chunks(o_ref.shape[1]):
        gate = _sigmoid(jnp.dot(xn, wg_ref[:, sl], preferred_element_type=F32))
        proj = jnp.dot(pb, wp_ref[:, sl], preferred_element_type=F32)
        o_ref[:, sl] = x_ref[:, sl] + gate * proj


def ple(x, g, p, w_gate, w_proj, layer, batch, *, tm=512):
    m, d = x.shape
    pd = p.shape[3]
    return pl.pallas_call(
        _ple_kernel,
        out_shape=jax.ShapeDtypeStruct((m, d), F32),
        grid=(m // tm,),
        in_specs=[
            pl.BlockSpec((tm, d), lambda i: (i, 0)),
            _const_spec((1, d)),
            pl.BlockSpec((None, None, tm, pd), lambda i: (layer, batch, i, 0)),
            _const_spec((d, d)),
            _const_spec((pd, d)),
        ],
        out_specs=pl.BlockSpec((tm, d), lambda i: (i, 0)),
        scratch_shapes=[pltpu.VMEM((tm, d), BF16)],
        compiler_params=_cparams("parallel"),
        name="ple",
    )(x, g.reshape(1, d), p, w_gate, w_proj)


def _s5_disc(lr, li, dt):
    mag = jnp.exp(lr * dt)
    ar = mag * jnp.cos(li * dt)
    ai = mag * jnp.sin(li * dt)
    den = lr * lr + li * li
    cr = ((ar - 1.0) * lr + ai * li) / den
    ci = (ai * lr - (ar - 1.0) * li) / den
    return ar, ai, cr, ci


def _s5_params_kernel(lr_ref, li_ref, ldt_ref, bre_ref, bim_ref, cre_ref, cim_ref,
                      ar_ref, ai_ref, bblk_ref, cblk_ref, *, nblk, gs, nst):
    ar, ai, cr, ci = _s5_disc(lr_ref[...], li_ref[...], jnp.exp(ldt_ref[...]))
    ar_ref[...] = ar
    ai_ref[...] = ai
    br = bre_ref[...]
    bi = bim_ref[...]
    rows, cols = bblk_ref.shape[1], bblk_ref.shape[2]
    assert gs & (gs - 1) == 0 and nst & (nst - 1) == 0
    same_group = (lax.broadcasted_iota(jnp.int32, (rows, cols), 0) >> (gs.bit_length() - 1)
                  == lax.broadcasted_iota(jnp.int32, (rows, cols), 1) >> (nst.bit_length() - 1))
    parts = ((bblk_ref, cr * br - ci * bi, cr * bi + ci * br, False),
             (cblk_ref, cre_ref[...], cim_ref[...], True))
    for dst, re, im, transposed in parts:
        for kb in range(nblk):
            for off, src in ((0, re), (nblk, im)):
                blk = src[:, kb * cols:(kb + 1) * cols]
                tiled = jnp.concatenate([blk] * (rows // gs), axis=0)
                val = jnp.where(same_group, tiled, 0.0)
                dst[off + kb] = (val.T if transposed else val).astype(dst.dtype)


def s5_params(lam_re, lam_im, log_dt, b_re, b_im, c_re, c_im):
    g, p = lam_re.shape
    gs = b_re.shape[-1]
    flat = lambda t: t.reshape(1, g * p)
    ldt = jnp.repeat(log_dt, p).reshape(1, g * p)
    b_t = lambda b: b.transpose(2, 0, 1).reshape(gs, g * p)
    c_t = lambda c: c.transpose(1, 0, 2).reshape(gs, g * p)
    gb = g // S5_NBLK
    shp = jax.ShapeDtypeStruct
    b_blocks = shp((2 * S5_NBLK, gb * gs, gb * p), BF16)
    c_blocks = shp((2 * S5_NBLK, gb * p, gb * gs), BF16)
    kern = functools.partial(_s5_params_kernel, nblk=S5_NBLK, gs=gs, nst=p)
    return pl.pallas_call(
        kern,
        out_shape=(shp((1, g * p), F32), shp((1, g * p), F32), b_blocks, c_blocks),
        name="s5_params",
    )(flat(lam_re), flat(lam_im), ldt, b_t(b_re), b_t(b_im), c_t(c_re), c_t(c_im))


S5_NBLK = 8
S5_T = 128
S5_PITCH = S5_T + 4


def _gelu_tanh(y):
    return 0.5 * y * (1.0 + jnp.tanh(math.sqrt(2.0 / math.pi) * (y + 0.044715 * (y * y * y))))


def _s5_scan_kernel(u_ref, bre_ref, bim_ref, cre_ref, cim_ref, ar_ref, ai_ref, d_ref, *rest, n_cast):
    cast_in, o_ref, cast_out = rest[:n_cast], rest[n_cast], rest[n_cast + 1:2 * n_cast + 1]
    xr_ref, xi_ref, sr_ref, si_ref = rest[2 * n_cast + 1:]
    br_ref, bi_ref = xr_ref, xi_ref
    _side_cast(cast_in, cast_out)
    nslab = xr_ref.shape[0]
    ucols = u_ref.shape[1] // S5_NBLK

    @pl.when(pl.program_id(0) == 0)
    def _():
        sr_ref[...] = jnp.zeros_like(sr_ref)
        si_ref[...] = jnp.zeros_like(si_ref)

    for kb in range(S5_NBLK):
        ukb = u_ref[:, kb * ucols:(kb + 1) * ucols].astype(BF16)
        bur = jnp.dot(ukb, bre_ref[kb], preferred_element_type=F32)
        bui = jnp.dot(ukb, bim_ref[kb], preferred_element_type=F32)
        for cs in range(nslab):
            br_ref[cs, kb * S5_PITCH:kb * S5_PITCH + S5_T, :] = bur[:, cs * LANES:(cs + 1) * LANES]
            bi_ref[cs, kb * S5_PITCH:kb * S5_PITCH + S5_T, :] = bui[:, cs * LANES:(cs + 1) * LANES]

    def step(t, carry):
        new = []
        for cs in range(nslab):
            sr, si = carry[cs]
            ar = ar_ref[:, cs * LANES:(cs + 1) * LANES]
            ai = ai_ref[:, cs * LANES:(cs + 1) * LANES]
            rows = pl.ds(t, S5_NBLK, stride=S5_PITCH)
            nr = ar * sr - ai * si + br_ref[cs, rows, :]
            ni = ar * si + ai * sr + bi_ref[cs, rows, :]
            xr_ref[cs, rows, :] = nr
            xi_ref[cs, rows, :] = ni
            new.append((nr, ni))
        return tuple(new)

    init = tuple((sr_ref[:, cs * LANES:(cs + 1) * LANES], si_ref[:, cs * LANES:(cs + 1) * LANES])
                 for cs in range(nslab))
    final = lax.fori_loop(0, S5_T, step, init, unroll=2)
    for cs in range(nslab):
        sr_ref[:, cs * LANES:(cs + 1) * LANES] = final[cs][0]
        si_ref[:, cs * LANES:(cs + 1) * LANES] = final[cs][1]

    for kb in range(S5_NBLK):
        rows = slice(kb * S5_PITCH, kb * S5_PITCH + S5_T)
        xr = jnp.concatenate([xr_ref[cs, rows, :] for cs in range(nslab)], axis=1).astype(BF16)
        xi = jnp.concatenate([xi_ref[cs, rows, :] for cs in range(nslab)], axis=1).astype(BF16)
        y = (jnp.dot(xr, cre_ref[kb], preferred_element_type=F32)
             - jnp.dot(xi, cim_ref[kb], preferred_element_type=F32))
        sl = slice(kb * ucols, (kb + 1) * ucols)
        y = y + d_ref[:, sl] * u_ref[:, sl]
        o_ref[:, sl] = _gelu_tanh(y)


def s5_scan(u, bblk, cblk, a_re, a_im, d_skip, casts=()):
    seq, d = u.shape
    half = lambda shape, which: pl.BlockSpec(shape, lambda c: (which, 0, 0), pipeline_mode=pl.Buffered(1))
    nstate = a_re.size
    blk_states = nstate // S5_NBLK
    nslab = blk_states // LANES
    ucols = d // S5_NBLK
    nc = seq // S5_T
    c_in, c_out, c_shapes = _side_cast_plan(casts, nc, lambda c: c)
    return pl.pallas_call(
        functools.partial(_s5_scan_kernel, n_cast=len(casts)),
        out_shape=[jax.ShapeDtypeStruct((seq, d), F32)] + c_shapes,
        grid=(nc,),
        in_specs=[
            pl.BlockSpec((S5_T, d), lambda c: (c, 0)),
            half((S5_NBLK, ucols, blk_states), 0),
            half((S5_NBLK, ucols, blk_states), 1),
            half((S5_NBLK, blk_states, ucols), 0),
            half((S5_NBLK, blk_states, ucols), 1),
            _const_spec((S5_NBLK, blk_states)),
            _const_spec((S5_NBLK, blk_states)),
            _const_spec((1, d)),
        ] + c_in,
        out_specs=[pl.BlockSpec((S5_T, d), lambda c: (c, 0))] + c_out,
        scratch_shapes=[
            pltpu.VMEM((nslab, S5_NBLK * S5_PITCH, LANES), F32),
            pltpu.VMEM((nslab, S5_NBLK * S5_PITCH, LANES), F32),
            pltpu.VMEM((S5_NBLK, blk_states), F32),
            pltpu.VMEM((S5_NBLK, blk_states), F32),
        ],
        compiler_params=_cparams("arbitrary"),
        name="s5_scan",
    )(u, bblk, bblk, cblk, cblk,
      a_re.reshape(S5_NBLK, blk_states), a_im.reshape(S5_NBLK, blk_states), d_skip.reshape(1, d),
      *[w for w, _ in casts])


def _glu_out_kernel(x_ref, g_ref, wglu_ref, wout_ref, o_ref, h_ref):
    gb = g_ref[...].astype(BF16)
    chunks = _col_chunks(o_ref.shape[1])
    for sl in chunks:
        gate = _sigmoid(jnp.dot(gb, wglu_ref[:, sl], preferred_element_type=F32))
        h_ref[:, sl] = (g_ref[:, sl] * gate).astype(BF16)
    h = h_ref[...]
    for sl in chunks:
        o_ref[:, sl] = x_ref[:, sl] + jnp.dot(h, wout_ref[:, sl], preferred_element_type=F32)


def glu_out(x, g, w_glu, w_out, *, tm=512):
    m, d = x.shape
    return pl.pallas_call(
        _glu_out_kernel,
        out_shape=jax.ShapeDtypeStruct((m, d), F32),
        grid=(m // tm,),
        in_specs=[
            pl.BlockSpec((tm, d), lambda i: (i, 0)),
            pl.BlockSpec((tm, d), lambda i: (i, 0)),
            _const_spec((d, d)),
            _const_spec((d, d)),
        ],
        out_specs=pl.BlockSpec((tm, d), lambda i: (i, 0)),
        scratch_shapes=[pltpu.VMEM((tm, d), BF16)],
        compiler_params=_cparams("parallel"),
        name="s5_glu_out",
    )(x, g, w_glu, w_out)


def _ml_inproj_kernel(x_ref, g_ref, w_ref, wg_ref, qkv_ref, o_ref, gates_ref, xn_ref):
    _rms_to(xn_ref, x_ref, g_ref)
    xn = xn_ref[...]
    n_qkv = qkv_ref.shape[1]
    gates_ref[...] = jnp.dot(xn, wg_ref[...], preferred_element_type=F32)
    for sl in _col_chunks(n_qkv):
        qkv_ref[:, sl] = jnp.dot(xn, w_ref[:, sl], preferred_element_type=F32).astype(qkv_ref.dtype)
    for sl in _col_chunks(o_ref.shape[1]):
        wcols = slice(n_qkv + sl.start, n_qkv + sl.stop)
        o_ref[:, sl] = jnp.dot(xn, w_ref[:, wcols], preferred_element_type=F32)


def ml_inproj(x, g, w, w_gates, n_qkv, n_o, *, tm=512):
    m, k = x.shape
    shp = jax.ShapeDtypeStruct
    return pl.pallas_call(
        _ml_inproj_kernel,
        out_shape=(shp((m, n_qkv), BF16), shp((m, n_o), F32), shp((m, LANES), F32)),
        grid=(m // tm,),
        in_specs=[
            pl.BlockSpec((tm, k), lambda i: (i, 0)),
            _const_spec((1, k)),
            _const_spec((k, w.shape[1])),
            _const_spec((k, LANES)),
        ],
        out_specs=(
            pl.BlockSpec((tm, n_qkv), lambda i: (i, 0)),
            pl.BlockSpec((tm, n_o), lambda i: (i, 0)),
            pl.BlockSpec((tm, LANES), lambda i: (i, 0)),
        ),
        scratch_shapes=[pltpu.VMEM((tm, k), BF16)],
        compiler_params=_cparams("parallel"),
        name="ml_inproj",
    )(x, g.reshape(1, k), w, w_gates)


ML_T = 128
ML_HEAD_GROUP = 4


def _mlstm_kernel(q_ref, k_ref, v_ref, o_ref, gc_ref, gr_ref, bc_ref, br_ref, hg_ref, *rest, n_cast):
    cast_in, out_ref, cast_out = rest[:n_cast], rest[n_cast], rest[n_cast + 1:2 * n_cast + 1]
    c_ref, m_ref = rest[2 * n_cast + 1:]
    _side_cast(cast_in, cast_out)
    t = ML_T
    nh = ML_HEADS
    dqk = q_ref.shape[1] // nh
    dv = v_ref.shape[1] // nh
    kscale = dqk ** -0.5

    @pl.when(pl.program_id(0) == 0)
    def _():
        c_ref[...] = jnp.zeros_like(c_ref)
        m_ref[...] = jnp.zeros_like(m_ref)

    row = lax.broadcasted_iota(jnp.int32, (t, t), 0)
    col = lax.broadcasted_iota(jnp.int32, (t, t), 1)
    causal = col <= row
    lower = causal.astype(F32)
    upper = (row <= col).astype(F32)
    eye = (lax.broadcasted_iota(jnp.int32, (dqk, dqk), 0)
           == lax.broadcasted_iota(jnp.int32, (dqk, dqk), 1)).astype(BF16)
    ones_col = (lax.broadcasted_iota(jnp.int32, (t, LANES), 1) == 0).astype(BF16)

    gcol = gc_ref[...] + bc_ref[...]
    lf_col = _log_sigmoid(gcol)
    bcum_col = jnp.dot(lower, lf_col, preferred_element_type=F32, precision=lax.Precision.HIGHEST)
    grow = gr_ref[...] + br_ref[...]
    lf_row = _log_sigmoid(grow)
    bcum_row = jnp.dot(lf_row, upper, preferred_element_type=F32, precision=lax.Precision.HIGHEST)

    m_all = m_ref[...]
    nt = (((1,), (1,)), ((), ()))

    def run_heads(heads):
        qs = {h: q_ref[:, h * dqk:(h + 1) * dqk] for h in heads}
        ks = {h: k_ref[:, h * dqk:(h + 1) * dqk] for h in heads}
        vs = {h: v_ref[:, h * dv:(h + 1) * dv] for h in heads}
        cs = {h: c_ref[h] for h in heads}
        bcs = {h: bcum_col[:, nh + h:nh + h + 1] for h in heads}
        brs = {h: bcum_row[nh + h:nh + h + 1, :] for h in heads}
        irs = {h: grow[h:h + 1, :] for h in heads}
        ms_ = {h: m_all[h:h + 1, 0:1] for h in heads}

        qk = {h: lax.dot_general(qs[h], ks[h], nt, preferred_element_type=F32) for h in heads}
        qc = {h: jnp.dot(qs[h], cs[h].astype(BF16), preferred_element_type=F32) for h in heads}
        kt = {h: lax.dot_general(eye, ks[h], nt, preferred_element_type=F32) for h in heads}

        dmat = {h: jnp.where(causal, bcs[h] - brs[h] + irs[h], -jnp.inf) for h in heads}
        inter = {h: bcs[h] + ms_[h] for h in heads}
        m_row = {h: jnp.maximum(jnp.max(dmat[h], axis=1, keepdims=True), inter[h]) for h in heads}
        s = {h: qk[h] * (jnp.exp(dmat[h] - m_row[h]) * kscale) for h in heads}
        w_inter = {h: jnp.exp(inter[h] - m_row[h]) for h in heads}
        sv = {h: jnp.dot(s[h].astype(BF16), vs[h], preferred_element_type=F32) for h in heads}

        b_last = {h: brs[h][:, t - 1:t] for h in heads}
        dec = {h: b_last[h] - brs[h] + irs[h] for h in heads}
        m_new = {h: jnp.maximum(b_last[h] + ms_[h], jnp.max(dec[h], axis=1, keepdims=True))
                 for h in heads}
        wk = {h: jnp.exp(dec[h] - m_new[h]) * kscale for h in heads}
        carry_scale = {h: jnp.exp(b_last[h] + ms_[h] - m_new[h]) for h in heads}
        for h in heads:
            v_ext = jnp.concatenate([vs[h], ones_col], axis=1)
            c_ref[h] = carry_scale[h] * cs[h] + jnp.dot((kt[h] * wk[h]).astype(BF16), v_ext,
                                                        preferred_element_type=F32)

        num = {h: sv[h] + w_inter[h] * qc[h][:, :dv] for h in heads}
        den = {h: jnp.sum(s[h], axis=1, keepdims=True) + w_inter[h] * qc[h][:, dv:dv + 1]
               for h in heads}
        h_t = {h: num[h] / jnp.maximum(jnp.abs(den[h]), jnp.exp(-m_row[h])) for h in heads}
        msq = {h: jnp.mean(h_t[h] * h_t[h], axis=1, keepdims=True) for h in heads}
        for h in heads:
            hn = h_t[h] * lax.rsqrt(msq[h] + EPS) * hg_ref[:, h * dv:(h + 1) * dv]
            og = _sigmoid(o_ref[:, h * dv:(h + 1) * dv])
            out_ref[:, h * dv:(h + 1) * dv] = (og * hn).astype(out_ref.dtype)
        return m_new

    m_new = {}
    for g0 in range(0, nh, ML_HEAD_GROUP):
        m_new.update(run_heads(range(g0, g0 + ML_HEAD_GROUP)))
    m_ref[...] = jnp.concatenate([jnp.broadcast_to(m_new[h], (1, m_ref.shape[1])) for h in range(nh)],
                                 axis=0)


def mlstm_scan(qkv, o, g_col, g_row, bias, head_gain, casts=()):
    seq = qkv.shape[0]
    d = o.shape[1]
    nqk = (qkv.shape[1] - d) // 2
    nh = ML_HEADS
    assert d == 2 * nqk
    bias_col = jnp.zeros((1, LANES), F32).at[0, :2 * nh].set(bias)
    bias_row = bias.reshape(2 * nh, 1)
    nc = seq // ML_T
    c_in, c_out, c_shapes = _side_cast_plan(casts, nc, lambda c: c)
    return pl.pallas_call(
        functools.partial(_mlstm_kernel, n_cast=len(casts)),
        out_shape=[jax.ShapeDtypeStruct((seq, d), BF16)] + c_shapes,
        grid=(nc,),
        in_specs=[
            pl.BlockSpec((ML_T, nqk), lambda c: (c, 0)),
            pl.BlockSpec((ML_T, nqk), lambda c: (c, 1)),
            pl.BlockSpec((ML_T, d), lambda c: (c, 1)),
            pl.BlockSpec((ML_T, d), lambda c: (c, 0)),
            pl.BlockSpec((ML_T, LANES), lambda c: (c, 0)),
            pl.BlockSpec((2 * nh, ML_T), lambda c: (0, c)),
            _const_spec((1, LANES)),
            _const_spec((2 * nh, 1)),
            _const_spec((1, d)),
        ] + c_in,
        out_specs=[pl.BlockSpec((ML_T, d), lambda c: (c, 0))] + c_out,
        scratch_shapes=[
            pltpu.VMEM((nh, nqk // nh, d // nh + LANES), F32),
            pltpu.VMEM((nh, LANES), F32),
        ],
        compiler_params=_cparams("arbitrary"),
        name="mlstm_scan",
    )(qkv, qkv, qkv, o, g_col, g_row, bias_col, bias_row, head_gain.reshape(1, d),
      *[w for w, _ in casts])


def _run_casts(core, casts, ready):
    out, *cast = core([(stack, ix) for _, stack, ix in casts])
    ready.update({key: w for (key, _, _), w in zip(casts, cast)})
    return out


def _sb_layer(x, g, w_in, q_gain, k_gain, out_key, casts, ready):
    qkv = sb_inproj(x, g, w_in, q_gain, k_gain)
    o = _run_casts(functools.partial(sb_attention, qkv), casts, ready)
    return resid_matmul(x, o, ready[out_key], name="sb_out")


def _s5_layer(x, g, w_in, lam_re, lam_im, log_dt, b_re, b_im, c_re, c_im, d_skip, glu_key, out_key,
              casts, ready):
    d = x.shape[1]
    u = norm_matmul(x, g, w_in, F32, tm=512, tn=d, name="s5_in")
    a_re, a_im, bblk, cblk = s5_params(lam_re, lam_im, log_dt, b_re, b_im, c_re, c_im)
    core = functools.partial(s5_scan, u, bblk, cblk, a_re, a_im, d_skip)
    y = _run_casts(core, casts, ready)
    return glu_out(x, y, ready[glu_key], ready[out_key])


def _mlstm_layer(x, g, w_in, gate_bias, head_gain, out_key, casts, ready):
    d = x.shape[1]
    nh = ML_HEADS
    nqk = d // 2
    n_qkv, n_o = 2 * nqk + d, d
    w_g = jnp.pad(w_in[:, n_qkv + n_o:], ((0, 0), (0, LANES - 2 * nh)))
    qkv, o, g_col = ml_inproj(x, g, w_in, w_g, n_qkv, n_o)
    g_row = g_col[:, :2 * nh].T
    core = functools.partial(mlstm_scan, qkv, o, g_col, g_row, gate_bias, head_gain)
    hs = _run_casts(core, casts, ready)
    return resid_matmul(x, hs, ready[out_key], name="ml_out")


def kernel(x, p, norm_mix, norm_ffn, sb_w_in, sb_q_gain, sb_k_gain, sb_w_out, s5_w_in, s5_lam_re, s5_lam_im, s5_log_dt, s5_b_re, s5_b_im, s5_c_re, s5_c_im, s5_d, s5_w_glu, s5_w_out, ml_w_in, ml_gate_bias, ml_head_gain, ml_w_out, ffn_w_gate, ffn_w_up, ffn_w_down, ple_norm, ple_w_gate, ple_w_proj):
    bsz, seq, d = x.shape
    depth = norm_mix.shape[0]
    mixer_w = {0: [("out", sb_w_out)], 1: [("glu", s5_w_glu), ("out", s5_w_out)], 2: [("out", ml_w_out)]}
    inproj_w = (sb_w_in, s5_w_in, ml_w_in)

    def needs(i):
        kind, j = i % 3, i // 3
        own = [((i, name), stack, j) for name, stack in mixer_w[kind]]
        shared = [((i, "ffn_gate"), ffn_w_gate, i), ((i, "ffn_up"), ffn_w_up, i),
                  ((i, "ffn_down"), ffn_w_down, i), ((i, "ple_gate"), ple_w_gate, i),
                  ((i, "ple_proj"), ple_w_proj, i)]
        nxt = []
        if i + 1 < depth and inproj_w[(i + 1) % 3].shape[-1] % LANES == 0:
            nxt = [((i + 1, "in"), inproj_w[(i + 1) % 3], (i + 1) // 3)]
        return own + shared + nxt

    host = {i: (i - 1 if i % 3 == 2 and i >= 1 else i) for i in range(depth)}
    casts = {i: [] for i in range(depth)}
    for i in range(depth):
        casts[host[i]] += needs(i)

    outs = []
    for b in range(bsz):
        h = x[b]
        ready = {}
        for i in range(depth):
            kind, j = i % 3, i // 3
            if (i, "in") not in ready:
                ready[(i, "in")] = inproj_w[kind][j].astype(BF16)
            w_in = ready[(i, "in")]
            if kind == 0:
                h = _sb_layer(h, norm_mix[i], w_in, sb_q_gain[j], sb_k_gain[j], (i, "out"), casts[i], ready)
            elif kind == 1:
                h = _s5_layer(h, norm_mix[i], w_in, s5_lam_re[j], s5_lam_im[j], s5_log_dt[j],
                              s5_b_re[j], s5_b_im[j], s5_c_re[j], s5_c_im[j], s5_d[j],
                              (i, "glu"), (i, "out"), casts[i], ready)
            else:
                h = _mlstm_layer(h, norm_mix[i], w_in, ml_gate_bias[j], ml_head_gain[j], (i, "out"),
                                 casts[i], ready)
            h = ffn(h, norm_ffn[i], ready[(i, "ffn_gate")], ready[(i, "ffn_up")], ready[(i, "ffn_down")])
            h = ple(h, ple_norm[i], p, ready[(i, "ple_gate")], ready[(i, "ple_proj")], i, b)
        outs.append(h)
    return jnp.stack(outs)
```
